```python
import math
import jax, jax.numpy as jnp
from jax import lax
import numpy as np

D_MODEL = 1024
BATCH = 16
SEQ = 2048
DEPTH = 4

N_MIXERS = 3
N_A_LAYERS = (DEPTH + 2) // 3
N_B_LAYERS = (DEPTH + 1) // 3
N_C_LAYERS = DEPTH // 3

RMS_EPS = 1e-6
CONV_W = 3

NA_HEAD_DIM = 64
NA_HEADS = D_MODEL // NA_HEAD_DIM
GRID_W = 64
NA_MAX_ROWS = 8
NA_COLS = 16
NA_QCOLS = 16
NA_SPAN = NA_QCOLS + NA_COLS

HYENA_ORDER = 2
HYENA_BANDS = 16
HYENA_EMB = 1 + 2 * HYENA_BANDS
HYENA_FILTER_HIDDEN = 64
HYENA_DECAY_TARGET = 1e-2
HYENA_FAST_DECAY = 0.3
HYENA_SLOW_DECAY = 1.5

FFN_HIDDEN = ((8 * D_MODEL + 3 * 256 - 1) // (3 * 256)) * 256

kernel_name = 'hybrid_shortconv_natten_hyena_encoder'


def rmsnorm(x, g):
    xf = x.astype(jnp.float32)
    y = xf * lax.rsqrt(jnp.mean(xf * xf, axis=-1, keepdims=True) + RMS_EPS)
    return (y * g.astype(jnp.float32)).astype(x.dtype)


def depthwise_conv3(x, w, b=None):
    c = x.shape[-1]
    y = lax.conv_general_dilated(
        x, w.astype(x.dtype)[:, None, :], (1,), ((CONV_W // 2, CONV_W // 2),),
        dimension_numbers=('NWC', 'WIO', 'NWC'), feature_group_count=c)
    return y if b is None else y + b.astype(x.dtype)


def short_conv_mixer(h, w_in, conv_w, w_out):
    b_gate, c_gate, u = jnp.split(h @ w_in, 3, axis=-1)
    return (b_gate * depthwise_conv3(c_gate * u, conv_w)) @ w_out


def na_column_tables():
    cb = np.arange(GRID_W // NA_QCOLS)
    starts = np.clip(cb * NA_QCOLS - NA_COLS // 2, 0, GRID_W - NA_SPAN)
    key_col = starts[:, None] + np.arange(NA_SPAN)
    q_col = cb[:, None] * NA_QCOLS + np.arange(NA_QCOLS)
    col_start = np.clip(q_col - NA_COLS // 2, 0, GRID_W - NA_COLS)
    kc = key_col[:, None, :]
    valid = (kc >= col_start[:, :, None]) & (kc < col_start[:, :, None] + NA_COLS)
    dc_idx = np.clip(kc - q_col[:, :, None] + NA_COLS - 1, 0, 2 * NA_COLS - 2)
    return key_col, valid, dc_idx


def neighborhood_attention(h, w_qkv, q_g, k_g, rpb, w_out):
    bsz, seq, d = h.shape
    rows = seq // GRID_W
    wr = min(NA_MAX_ROWS, rows)
    q, k, v = jnp.split(h @ w_qkv, 3, axis=-1)
    grid = (bsz, rows, GRID_W, NA_HEADS, NA_HEAD_DIM)
    q = rmsnorm(q.reshape(grid), q_g)
    k = rmsnorm(k.reshape(grid), k_g)
    v = v.reshape(grid)
    key_col, valid, dc_idx = na_column_tables()
    ncb = key_col.shape[0]
    scale = NA_HEAD_DIM ** -0.5
    neg = jnp.finfo(jnp.float32).min
    mask = valid[None, None, :, :, None, :]

    def row_block(r):
        rs = jnp.clip(r - wr // 2, 0, rows - wr)
        q_r = lax.dynamic_index_in_dim(q, r, axis=1, keepdims=False)
        q_r = q_r.reshape(bsz, ncb, NA_QCOLS, NA_HEADS, NA_HEAD_DIM)
        k_r = lax.dynamic_slice_in_dim(k, rs, wr, axis=1)[:, :, key_col]
        v_r = lax.dynamic_slice_in_dim(v, rs, wr, axis=1)[:, :, key_col]
        s = jnp.einsum('bnqhd,brnkhd->bhnqrk', q_r, k_r).astype(jnp.float32) * scale
        dr_idx = rs + jnp.arange(wr) - r + NA_MAX_ROWS - 1
        bias = rpb[:, dr_idx[None, None, :, None], dc_idx[:, :, None, :]]
        s = jnp.where(mask, s + bias.astype(jnp.float32)[None], neg)
        p = jax.nn.softmax(s.reshape(s.shape[:4] + (wr * NA_SPAN,)), axis=-1)
        p = p.reshape(s.shape).astype(v.dtype)
        o = jnp.einsum('bhnqrk,brnkhd->bnqhd', p, v_r)
        return o.reshape(bsz, GRID_W, d)

    o = lax.map(row_block, jnp.arange(rows))
    return o.transpose(1, 0, 2, 3).reshape(bsz, seq, d) @ w_out


def hyena_filters(seq, w1, b1, w2, b2, w3, freq):
    f32 = jnp.float32
    t = jnp.linspace(0.0, 1.0, seq, dtype=f32)[:, None]
    bands = jnp.linspace(1e-4, HYENA_BANDS - 1, HYENA_BANDS, dtype=f32)
    ang = (2.0 * math.pi) * jnp.arange(seq, dtype=f32)[:, None] / seq * bands
    z = jnp.concatenate([t, jnp.cos(ang), -jnp.sin(ang)], axis=-1)
    fr = freq.astype(f32)
    hid = jnp.sin(fr * (z @ w1.astype(f32) + b1.astype(f32)))
    hid = jnp.sin(fr * (hid @ w2.astype(f32) + b2.astype(f32)))
    filt = (hid @ w3.astype(f32)).reshape(seq, HYENA_ORDER, 2, D_MODEL)
    lt = math.log(HYENA_DECAY_TARGET)
    deltas = jnp.abs(jnp.linspace(lt / HYENA_SLOW_DECAY, lt / HYENA_FAST_DECAY, D_MODEL, dtype=f32))
    filt = filt * jnp.exp(-t * deltas)[:, None, None, :]
    fwd, rev = filt[:, :, 0], filt[:, :, 1]
    h_full = jnp.concatenate([fwd[:1] + rev[:1], fwd[1:], jnp.zeros_like(fwd[:1]), rev[:0:-1]], axis=0)
    h_full = h_full / jnp.sum(jnp.abs(h_full), axis=0, keepdims=True)
    return jnp.fft.rfft(h_full, axis=0)


def hyena_mixer(h, w_in, short_w, short_b, f_w1, f_b1, f_w2, f_b2, f_w3, f_freq, f_skip, w_out):
    seq = h.shape[1]
    v, x1, x2 = jnp.split(depthwise_conv3(h @ w_in, short_w, short_b), 3, axis=-1)
    h_freq = hyena_filters(seq, f_w1, f_b1, f_w2, f_b2, f_w3, f_freq)

    def long_conv(u, n):
        uf = u.astype(jnp.float32)
        y = jnp.fft.irfft(jnp.fft.rfft(uf, n=2 * seq, axis=1) * h_freq[:, n], n=2 * seq, axis=1)[:, :seq]
        return (y + uf * f_skip[n].astype(jnp.float32)).astype(u.dtype)

    z = x1 * long_conv(v, 0)
    z = x2 * long_conv(z, 1)
    return z @ w_out


def swiglu(h, w13, w2):
    g, u = jnp.split(h @ w13, 2, axis=-1)
    return (jax.nn.silu(g) * u) @ w2


def setup_inputs(seed: int = 0) -> dict:
    key = jax.random.key(seed)
    ks = jax.random.split(key, 24)
    f32 = jnp.float32

    def nrm(k, shape, fan_in):
        return jax.random.normal(k, shape, f32) * (fan_in ** -0.5)

    def gain(k, shape):
        return 1.0 + 0.02 * jax.random.normal(k, shape, f32)

    def small(k, shape, s=0.02):
        return s * jax.random.normal(k, shape, f32)

    D = D_MODEL
    return {
        'x': jax.random.normal(ks[0], (BATCH, SEQ, D), f32),
        'norm_mix_g': gain(ks[1], (DEPTH, D)),
        'norm_ffn_g': gain(ks[2], (DEPTH, D)),
        'a_w_in': nrm(ks[3], (N_A_LAYERS, D, 3 * D), D),
        'a_conv_w': nrm(ks[4], (N_A_LAYERS, CONV_W, D), CONV_W),
        'a_w_out': nrm(ks[5], (N_A_LAYERS, D, D), D),
        'b_w_qkv': nrm(ks[6], (N_B_LAYERS, D, 3 * D), D),
        'b_q_norm_g': gain(ks[7], (N_B_LAYERS, NA_HEAD_DIM)),
        'b_k_norm_g': gain(ks[8], (N_B_LAYERS, NA_HEAD_DIM)),
        'b_rpb': small(ks[9], (N_B_LAYERS, NA_HEADS, 2 * NA_MAX_ROWS - 1, 2 * NA_COLS - 1)),
        'b_w_out': nrm(ks[10], (N_B_LAYERS, D, D), D),
        'c_w_in': nrm(ks[11], (N_C_LAYERS, D, 3 * D), D),
        'c_short_w': nrm(ks[12], (N_C_LAYERS, CONV_W, 3 * D), CONV_W),
        'c_short_b': small(ks[13], (N_C_LAYERS, 3 * D)),
        'c_f_w1': nrm(ks[14], (N_C_LAYERS, HYENA_EMB, HYENA_FILTER_HIDDEN), HYENA_EMB),
        'c_f_b1': small(ks[15], (N_C_LAYERS, HYENA_FILTER_HIDDEN)),
        'c_f_w2': nrm(ks[16], (N_C_LAYERS, HYENA_FILTER_HIDDEN, HYENA_FILTER_HIDDEN), HYENA_FILTER_HIDDEN),
        'c_f_b2': small(ks[17], (N_C_LAYERS, HYENA_FILTER_HIDDEN)),
        'c_f_w3': nrm(ks[18], (N_C_LAYERS, HYENA_FILTER_HIDDEN, HYENA_ORDER * 2 * D), HYENA_FILTER_HIDDEN),
        'c_f_freq': gain(ks[19], (N_C_LAYERS, HYENA_FILTER_HIDDEN)),
        'c_f_skip': small(ks[20], (N_C_LAYERS, HYENA_ORDER, D), 0.5),
        'c_w_out': nrm(ks[21], (N_C_LAYERS, D, D), D),
        'f_w13': nrm(ks[22], (DEPTH, D, 2 * FFN_HIDDEN), D),
        'f_w2': nrm(ks[23], (DEPTH, FFN_HIDDEN, D), FFN_HIDDEN),
    }


def reference(x, norm_mix_g, norm_ffn_g, a_w_in, a_conv_w, a_w_out,
              b_w_qkv, b_q_norm_g, b_k_norm_g, b_rpb, b_w_out,
              c_w_in, c_short_w, c_short_b, c_f_w1, c_f_b1, c_f_w2, c_f_b2,
              c_f_w3, c_f_freq, c_f_skip, c_w_out, f_w13, f_w2):
    ia = ib = ic = 0
    for i in range(DEPTH):
        h = rmsnorm(x, norm_mix_g[i])
        kind = i % N_MIXERS
        if kind == 0:
            y = short_conv_mixer(h, a_w_in[ia], a_conv_w[ia], a_w_out[ia])
            ia += 1
        elif kind == 1:
            y = neighborhood_attention(h, b_w_qkv[ib], b_q_norm_g[ib], b_k_norm_g[ib], b_rpb[ib], b_w_out[ib])
            ib += 1
        else:
            y = hyena_mixer(h, c_w_in[ic], c_short_w[ic], c_short_b[ic], c_f_w1[ic], c_f_b1[ic],
                            c_f_w2[ic], c_f_b2[ic], c_f_w3[ic], c_f_freq[ic], c_f_skip[ic], c_w_out[ic])
            ic += 1
        x = x + y
        x = x + swiglu(rmsnorm(x, norm_ffn_g[i]), f_w13[i], f_w2[i])
    return x
```

```python
import functools
import math

import jax
import jax.numpy as jnp
from jax import lax
from jax.experimental import pallas as pl
from jax.experimental.pallas import tpu as pltpu

F32 = jnp.float32
BF16 = jnp.bfloat16
HIGHEST = lax.Precision.HIGHEST

RMS_EPS = 1e-6
GRID_W = 64
NA_HEAD_DIM = 64
NA_WIN_ROWS = 8
NA_WIN_COLS = 16
NA_RPB_ROWS = 2 * NA_WIN_ROWS - 1
NA_RPB_COLS = 2 * NA_WIN_COLS - 1
HYENA_BANDS = 16
HYENA_DECAY_TARGET = 1e-2
HYENA_FAST_DECAY = 0.3
HYENA_SLOW_DECAY = 1.5
MASK_VALUE = -1e30

V7X_LANES = 128
V7X_MXU_DIM = 256
V7X_BF16_SUBLANES = 16
VMEM_LIMIT_BYTES = 56 * 1024 * 1024


def _params(*semantics):
    return pltpu.CompilerParams(dimension_semantics=semantics,
                                vmem_limit_bytes=VMEM_LIMIT_BYTES)


def _rmsnorm_bf16(x, g):
    ms = jnp.mean(x * x, axis=-1, keepdims=True)
    return (x * lax.rsqrt(ms + RMS_EPS) * g).astype(BF16)


def _dot(a, b):
    return jnp.dot(a, b, preferred_element_type=F32)


def _norm_matmul_kernel(x_ref, g_ref, w_ref, hg_ref, o_ref, *, qk_cols):
    xn = _rmsnorm_bf16(x_ref[...], g_ref[...])
    n_total = w_ref.shape[1]
    nc = V7X_MXU_DIM
    if qk_cols:
        r = lax.broadcasted_iota(jnp.int32, (nc, nc), 0) // NA_HEAD_DIM
        c = lax.broadcasted_iota(jnp.int32, (nc, nc), 1) // NA_HEAD_DIM
        head_ones = jnp.where(r == c, 1.0, 0.0).astype(BF16)
    for n in range(0, n_total, nc):
        acc = _dot(xn, w_ref[:, n:n + nc])
        if n < qk_cols:
            ssq = _dot((acc * acc).astype(BF16), head_ones)
            acc = acc * lax.rsqrt(ssq * (1.0 / NA_HEAD_DIM) + RMS_EPS) * hg_ref[:, n:n + nc]
        o_ref[:, n:n + nc] = acc.astype(o_ref.dtype)


def _norm_matmul(x, g, w, head_gain=None, *, tm=512):
    t, d = x.shape
    n = w.shape[1]
    qk_cols = 0
    if head_gain is None:
        head_gain = jnp.ones((1, V7X_LANES), F32)
    else:
        qk_cols = head_gain.shape[1]
    return pl.pallas_call(
        functools.partial(_norm_matmul_kernel, qk_cols=qk_cols),
        grid=(t // tm,),
        in_specs=[
            pl.BlockSpec((tm, d), lambda i: (i, 0)),
            pl.BlockSpec((1, d), lambda i: (0, 0)),
            pl.BlockSpec((d, n), lambda i: (0, 0)),
            pl.BlockSpec(head_gain.shape, lambda i: (0, 0)),
        ],
        out_specs=pl.BlockSpec((tm, n), lambda i: (i, 0)),
        out_shape=jax.ShapeDtypeStruct((t, n), BF16),
        compiler_params=_params("parallel"),
    )(x, g, w, head_gain)


def _matmul_residual_kernel(a_ref, w_ref, x_ref, o_ref):
    o_ref[...] = x_ref[...] + _dot(a_ref[...], w_ref[...])


def _matmul_residual(a, w, x, *, tm=512):
    t, k = a.shape
    n = w.shape[1]
    return pl.pallas_call(
        _matmul_residual_kernel,
        grid=(t // tm,),
        in_specs=[
            pl.BlockSpec((tm, k), lambda i: (i, 0)),
            pl.BlockSpec((k, n), lambda i: (0, 0)),
            pl.BlockSpec((tm, n), lambda i: (i, 0)),
        ],
        out_specs=pl.BlockSpec((tm, n), lambda i: (i, 0)),
        out_shape=jax.ShapeDtypeStruct((t, n), F32),
        compiler_params=_params("parallel"),
    )(a, w, x)


def _ffn_kernel(x_ref, g_ref, w13_ref, w2_ref, o_ref, acc_ref, *, hc):
    x = x_ref[...]
    xn = _rmsnorm_bf16(x, g_ref[...])
    hidden = w2_ref.shape[0]
    for c in range(0, hidden, hc):
        gate = _dot(xn, w13_ref[:, c:c + hc])
        up = _dot(xn, w13_ref[:, hidden + c:hidden + c + hc])
        act = (gate * jax.nn.sigmoid(gate) * up).astype(BF16)
        part = _dot(act, w2_ref[c:c + hc, :])
        if c == 0:
            acc_ref[...] = part
        else:
            acc_ref[...] += part
    o_ref[...] = x + acc_ref[...]


def _ffn(x, g, w13, w2, *, tm=512, hc=256):
    t, d = x.shape
    hidden = w2.shape[0]
    return pl.pallas_call(
        functools.partial(_ffn_kernel, hc=hc),
        grid=(t // tm,),
        in_specs=[
            pl.BlockSpec((tm, d), lambda i: (i, 0)),
            pl.BlockSpec((1, d), lambda i: (0, 0)),
            pl.BlockSpec((d, 2 * hidden), lambda i: (0, 0)),
            pl.BlockSpec((hidden, d), lambda i: (0, 0)),
        ],
        out_specs=pl.BlockSpec((tm, d), lambda i: (i, 0)),
        out_shape=jax.ShapeDtypeStruct((t, d), F32),
        scratch_shapes=[pltpu.VMEM((tm, d), F32)],
        compiler_params=_params("parallel"),
    )(x, g, w13, w2)


def _conv3_rows(v, prev_row, next_row, w_ref):
    tm = v.shape[0]
    rows = lax.broadcasted_iota(jnp.int32, (tm, 1), 0)
    v_prev = jnp.where(rows == 0, prev_row, pltpu.roll(v, 1, axis=0))
    v_next = jnp.where(rows == tm - 1, next_row, pltpu.roll(v, tm - 1, axis=0))
    return w_ref[0:1, :] * v_prev + w_ref[1:2, :] * v + w_ref[2:3, :] * v_next


def _seq_edges(tiles_per_seq):
    i = pl.program_id(0) % tiles_per_seq
    return i == 0, i == tiles_per_seq - 1


def _halo_specs(tm, cols, col_index, t):
    hb = V7X_BF16_SUBLANES
    n_hblocks = t // hb
    per_tile = tm // hb
    prev_spec = pl.BlockSpec(
        (hb, cols), lambda i, *j: (jnp.maximum(i * per_tile - 1, 0), col_index(*j)))
    next_spec = pl.BlockSpec(
        (hb, cols), lambda i, *j: (jnp.minimum((i + 1) * per_tile, n_hblocks - 1), col_index(*j)))
    return prev_spec, next_spec


def _sconv_out_kernel(p_ref, pp_ref, pn_ref, x_ref, cw_ref, wo_ref, o_ref, *, tiles_per_seq):
    d = x_ref.shape[1]
    hb = pp_ref.shape[0]
    first, last = _seq_edges(tiles_per_seq)
    v = p_ref[:, d:2 * d].astype(F32) * p_ref[:, 2 * d:3 * d].astype(F32)
    vp = pp_ref[:, d:2 * d].astype(F32) * pp_ref[:, 2 * d:3 * d].astype(F32)
    vn = pn_ref[:, d:2 * d].astype(F32) * pn_ref[:, 2 * d:3 * d].astype(F32)
    prev_row = jnp.where(first, 0.0, vp[hb - 1:hb, :])
    next_row = jnp.where(last, 0.0, vn[0:1, :])
    conv = _conv3_rows(v, prev_row, next_row, cw_ref)
    y = (p_ref[:, 0:d].astype(F32) * conv).astype(BF16)
    o_ref[...] = x_ref[...] + _dot(y, wo_ref[...])


def _sconv_out(p, x, conv_w, w_out, seq, *, tm=512):
    t, d = x.shape
    prev_spec, next_spec = _halo_specs(tm, 3 * d, lambda: 0, t)
    return pl.pallas_call(
        functools.partial(_sconv_out_kernel, tiles_per_seq=seq // tm),
        grid=(t // tm,),
        in_specs=[
            pl.BlockSpec((tm, 3 * d), lambda i: (i, 0)),
            prev_spec, next_spec,
            pl.BlockSpec((tm, d), lambda i: (i, 0)),
            pl.BlockSpec((3, d), lambda i: (0, 0)),
            pl.BlockSpec((d, d), lambda i: (0, 0)),
        ],
        out_specs=pl.BlockSpec((tm, d), lambda i: (i, 0)),
        out_shape=jax.ShapeDtypeStruct((t, d), F32),
        compiler_params=_params("parallel"),
    )(p, p, p, x, conv_w, w_out)


def _conv3_bias_kernel(p_ref, pp_ref, pn_ref, cw_ref, cb_ref, o_ref, *, tiles_per_seq):
    hb = pp_ref.shape[0]
    first, last = _seq_edges(tiles_per_seq)
    prev_row = jnp.where(first, 0.0, pp_ref[...].astype(F32)[hb - 1:hb, :])
    next_row = jnp.where(last, 0.0, pn_ref[...].astype(F32)[0:1, :])
    conv = _conv3_rows(p_ref[...].astype(F32), prev_row, next_row, cw_ref)
    o_ref[...] = (conv + cb_ref[...]).astype(o_ref.dtype)


def _conv3_bias(p, conv_w, conv_b, seq, *, tm=512, tn=1024):
    t, n = p.shape
    prev_spec, next_spec = _halo_specs(tm, tn, lambda j: j, t)
    return pl.pallas_call(
        functools.partial(_conv3_bias_kernel, tiles_per_seq=seq // tm),
        grid=(t // tm, n // tn),
        in_specs=[
            pl.BlockSpec((tm, tn), lambda i, j: (i, j)),
            prev_spec, next_spec,
            pl.BlockSpec((3, tn), lambda i, j: (0, j)),
            pl.BlockSpec((1, tn), lambda i, j: (0, j)),
        ],
        out_specs=pl.BlockSpec((tm, tn), lambda i, j: (i, j)),
        out_shape=jax.ShapeDtypeStruct((t, n), BF16),
        compiler_params=_params("parallel", "parallel"),
    )(p, p, p, conv_w, conv_b)


def _bias_table_kernel(r_ref, o_ref):
    kdim = r_ref.shape[1]
    n = o_ref.shape[1]

    def decode(shape):
        c = lax.broadcasted_iota(jnp.int32, shape, 1)
        qc = c // (2 * GRID_W)
        lane = c % (2 * GRID_W)
        second = lane >= GRID_W
        kc = jnp.where(second, lane - GRID_W, lane)
        start = jnp.clip(qc - NA_WIN_COLS // 2, 0, GRID_W - NA_WIN_COLS)
        valid = (kc >= start) & (kc < start + NA_WIN_COLS)
        idx = kc - qc + (NA_WIN_COLS - 1) + jnp.where(second, NA_RPB_COLS, 0)
        return valid, idx

    valid, idx = decode((kdim, n))
    i = lax.broadcasted_iota(jnp.int32, (kdim, n), 0)
    onehot = jnp.where(valid & (i == idx), 1.0, 0.0)
    tab = jnp.dot(r_ref[...], onehot, preferred_element_type=F32, precision=HIGHEST)
    valid_row, _ = decode((1, n))
    o_ref[...] = jnp.where(valid_row, tab, MASK_VALUE)


def _bias_table(rpb):
    heads = rpb.shape[0]
    pairs = NA_RPB_ROWS - 1
    kdim = 64
    pad = jnp.zeros((heads, pairs, kdim - 2 * NA_RPB_COLS), F32)
    r = jnp.concatenate([rpb[:, :-1, :], rpb[:, 1:, :], pad], axis=-1).reshape(heads * pairs, kdim)
    n = GRID_W * 2 * GRID_W
    tab = pl.pallas_call(
        _bias_table_kernel,
        out_shape=jax.ShapeDtypeStruct((heads * pairs, n), F32),
        compiler_params=pltpu.CompilerParams(vmem_limit_bytes=VMEM_LIMIT_BYTES),
    )(r)
    return tab.reshape(heads, pairs, GRID_W, 2 * GRID_W)


def _natten_kernel(q_ref, k_ref, v_ref, tab_ref, o_ref, *, rows_per_step, n_rows):
    heads = tab_ref.shape[0]
    lane = lax.broadcasted_iota(jnp.int32, (GRID_W, V7X_LANES), 1)
    win = NA_WIN_ROWS * GRID_W

    def row_body(rr, carry):
        r = pl.program_id(1) * rows_per_step + rr
        rs = jnp.clip(r - NA_WIN_ROWS // 2, 0, n_rows - NA_WIN_ROWS)
        d0 = rs - r + NA_WIN_ROWS - 1
        q_rows = pl.ds(pl.multiple_of(rr * GRID_W, GRID_W), GRID_W)
        k_rows = pl.ds(pl.multiple_of(rs * GRID_W, GRID_W), win)
        for hp in range(heads // 2):
            cols = slice(hp * V7X_LANES, (hp + 1) * V7X_LANES)
            q2 = q_ref[0, q_rows, cols]
            k2 = k_ref[0, k_rows, cols]
            v2 = v_ref[0, k_rows, cols]
            out = None
            for hh in range(2):
                h = 2 * hp + hh
                in_head = (lane >= NA_HEAD_DIM) if hh else (lane < NA_HEAD_DIM)
                qm = jnp.where(in_head, q2, jnp.zeros_like(q2))
                s = lax.dot_general(qm, k2, (((1,), (1,)), ((), ())),
                                    preferred_element_type=F32)
                bias = jnp.concatenate(
                    [tab_ref[h, pl.ds(d0 + 2 * p, 1)][0] for p in range(NA_WIN_ROWS // 2)], axis=1)
                s = s + bias
                m = jnp.max(s, axis=-1, keepdims=True)
                e = jnp.exp(s - m)
                denom = jnp.sum(e, axis=-1, keepdims=True)
                o = _dot(e.astype(BF16), v2) * (1.0 / denom)
                out = o if out is None else jnp.where(in_head, o, out)
            o_ref[0, q_rows, cols] = out.astype(o_ref.dtype)
        return carry

    lax.fori_loop(0, rows_per_step, row_body, 0)


def _natten(qkv, tab, *, rows_per_step=8):
    b, s, d3 = qkv.shape
    d = d3 // 3
    n_rows = s // GRID_W
    qs = rows_per_step * GRID_W
    return pl.pallas_call(
        functools.partial(_natten_kernel, rows_per_step=rows_per_step, n_rows=n_rows),
        grid=(b, n_rows // rows_per_step),
        in_specs=[
            pl.BlockSpec((1, qs, d), lambda i, j: (i, j, 0)),
            pl.BlockSpec((1, s, d), lambda i, j: (i, 0, 1)),
            pl.BlockSpec((1, s, d), lambda i, j: (i, 0, 2)),
            pl.BlockSpec(tab.shape, lambda i, j: (0, 0, 0, 0)),
        ],
        out_specs=pl.BlockSpec((1, qs, d), lambda i, j: (i, j, 0)),
        out_shape=jax.ShapeDtypeStruct((b, s, d), BF16),
        compiler_params=_params("parallel", "arbitrary"),
    )(qkv, qkv, qkv, tab)


def _dft_matrices(seq):
    n = 2 * seq
    root = int(math.isqrt(n))
    row = jnp.arange(n, dtype=jnp.int32)
    keff = jnp.where(row <= seq, row, row - seq)
    quarter = jnp.where(row > seq, n // 4, 0)
    hi = root * jnp.arange(root, dtype=jnp.int32)
    lo = jnp.arange(root, dtype=jnp.int32)
    step = 2.0 * math.pi / n
    ang_hi = ((keff[:, None] * hi[None, :]) % n).astype(F32) * step
    ang_lo = ((keff[:, None] * lo[None, :] + quarter[:, None]) % n).astype(F32) * step
    c_hi, s_hi, c_lo, s_lo = jnp.cos(ang_hi), jnp.sin(ang_hi), jnp.cos(ang_lo), jnp.sin(ang_lo)
    fwd = c_hi[:, :, None] * c_lo[:, None, :] - s_hi[:, :, None] * s_lo[:, None, :]
    fwd = fwd.reshape(n, n).astype(BF16)
    half = root // 2
    inv = (c_hi.T[:half, None, :] * c_lo.T[None, :, :] - s_hi.T[:half, None, :] * s_lo.T[None, :, :])
    inv = inv.reshape(seq, n).astype(BF16)
    return fwd, inv


def _filter_kernel(bands_ref, w1t_ref, w1c_ref, w1s_ref, b1_ref, w2_ref, b2_ref, freq_ref,
                   w3f_ref, w3r_ref, delta_ref, h_ref):
    seq = h_ref.shape[0] // 2
    j = lax.broadcasted_iota(jnp.int32, (seq, 1), 0)

    def mlp(idx):
        pos = idx.astype(F32)
        t = pos / (seq - 1.0)
        ang = (2.0 * math.pi) * pos / seq * bands_ref[...]
        pre = (t * w1t_ref[...]
               + jnp.dot(jnp.cos(ang), w1c_ref[...], preferred_element_type=F32, precision=HIGHEST)
               + jnp.dot(-jnp.sin(ang), w1s_ref[...], preferred_element_type=F32, precision=HIGHEST)
               + b1_ref[...])
        hid = jnp.sin(freq_ref[...] * pre)
        hid = jnp.sin(freq_ref[...] * (
            jnp.dot(hid, w2_ref[...], preferred_element_type=F32, precision=HIGHEST) + b2_ref[...]))
        return hid, t

    hid_f, t_f = mlp(j)
    hid_r, t_r = mlp(jnp.where(j == 0, 0, seq - j))
    fwd = jnp.dot(hid_f, w3f_ref[...], preferred_element_type=F32, precision=HIGHEST)
    fwd = fwd * jnp.exp(-t_f * delta_ref[...])
    rev = jnp.dot(hid_r, w3r_ref[...], preferred_element_type=F32, precision=HIGHEST)
    rev = rev * jnp.exp(-t_r * delta_ref[...])
    top = fwd + jnp.where(j == 0, rev, 0.0)
    bot = jnp.where(j == 0, 0.0, rev)
    l1 = jnp.sum(jnp.abs(top), axis=0, keepdims=True) + jnp.sum(jnp.abs(bot), axis=0, keepdims=True)
    inv = 1.0 / l1
    h_ref[0:seq, :] = (top * inv).astype(h_ref.dtype)
    h_ref[seq:2 * seq, :] = (bot * inv).astype(h_ref.dtype)


def _hyena_filters_time(seq, d, w1, b1, w2, b2, w3, freq, *, tn=512):
    nb = HYENA_BANDS
    hid = w2.shape[0]
    bands = jnp.linspace(1e-4, nb - 1, nb, dtype=F32)[None, :]
    lt = math.log(HYENA_DECAY_TARGET)
    deltas = jnp.abs(jnp.linspace(lt / HYENA_SLOW_DECAY, lt / HYENA_FAST_DECAY, d, dtype=F32))[None, :]
    small = lambda shape: pl.BlockSpec(shape, lambda o, c: (0, 0))
    cpo = d // tn
    return pl.pallas_call(
        _filter_kernel,
        grid=(2, cpo),
        in_specs=[
            small((1, nb)), small((1, hid)), small((nb, hid)), small((nb, hid)), small((1, hid)),
            small((hid, hid)), small((1, hid)), small((1, hid)),
            pl.BlockSpec((hid, tn), lambda o, c: (0, 2 * o * cpo + c)),
            pl.BlockSpec((hid, tn), lambda o, c: (0, (2 * o + 1) * cpo + c)),
            pl.BlockSpec((1, tn), lambda o, c: (0, c)),
        ],
        out_specs=pl.BlockSpec((2 * seq, tn), lambda o, c: (0, o * cpo + c)),
        out_shape=jax.ShapeDtypeStruct((2 * seq, 2 * d), BF16),
        compiler_params=_params("parallel", "parallel"),
    )(bands, w1[0:1], w1[1:1 + nb], w1[1 + nb:1 + 2 * nb], b1[None, :], w2, b2[None, :],
      freq[None, :], w3, w3, deltas)


def _matmul_kernel(a_ref, b_ref, o_ref):
    o_ref[...] = _dot(a_ref[...], b_ref[...]).astype(o_ref.dtype)


def _matmul(a, b, out_dtype, *, tm=512, tn=512):
    m, k = a.shape
    n = b.shape[1]
    return pl.pallas_call(
        _matmul_kernel,
        grid=(m // tm, n // tn),
        in_specs=[pl.BlockSpec((tm, k), lambda i, j: (i, 0)),
                  pl.BlockSpec((k, tn), lambda i, j: (0, j))],
        out_specs=pl.BlockSpec((tm, tn), lambda i, j: (i, j)),
        out_shape=jax.ShapeDtypeStruct((m, n), out_dtype),
        compiler_params=_params("parallel", "parallel"),
    )(a, b)


def _dft_filter_kernel(c_ref, s_ref, u_ref, hre_ref, him_ref, y_ref):
    tk = c_ref.shape[0]
    n = 2 * c_ref.shape[1]
    u = u_ref[...]
    ure = _dot(c_ref[...], u)
    uim = _dot(s_ref[...], u)
    hre = hre_ref[...]
    him = him_ref[...]
    rows = pl.program_id(0) * tk + lax.broadcasted_iota(jnp.int32, (tk, 1), 0)
    packed = rows == 0
    yre = jnp.where(packed, 0.5 * ure * hre, ure * hre - uim * him) * (2.0 / n)
    yim = jnp.where(packed, 0.5 * uim * him, ure * him + uim * hre) * (2.0 / n)
    y_ref[0, 0] = yre.astype(y_ref.dtype)
    y_ref[0, 1] = yim.astype(y_ref.dtype)


def _dft_filter(fwd, u, u_col, spec, spec_col, batch, seq, d, *, tk=512, tn=512):
    kb = seq // tk
    nb = d // tn
    return pl.pallas_call(
        _dft_filter_kernel,
        grid=(kb, nb, batch),
        in_specs=[
            pl.BlockSpec((tk, seq), lambda k, j, b: (k, 0)),
            pl.BlockSpec((tk, seq), lambda k, j, b: (kb + k, 0)),
            pl.BlockSpec((seq, tn), lambda k, j, b: (b, u_col * nb + j)),
            pl.BlockSpec((tk, tn), lambda k, j, b: (k, spec_col * nb + j)),
            pl.BlockSpec((tk, tn), lambda k, j, b: (kb + k, spec_col * nb + j)),
        ],
        out_specs=pl.BlockSpec((1, 2, tk, tn), lambda k, j, b: (b, 0, k, j)),
        out_shape=jax.ShapeDtypeStruct((batch, 2, seq, d), BF16),
        compiler_params=_params("parallel", "parallel", "arbitrary"),
    )(fwd, fwd, u, spec, spec)


def _idft_gate_kernel(g_ref, y_ref, u_ref, gate_ref, skip_ref, o_ref):
    y = _dot(g_ref[...], y_ref[0])
    u = u_ref[...].astype(F32)
    o_ref[...] = (gate_ref[...].astype(F32) * (y + u * skip_ref[...])).astype(o_ref.dtype)


def _idft_gate(inv, y, u, u_col, gate, gate_col, skip, batch, seq, d, *, tm=512, tn=512):
    mb = seq // tm
    nb = d // tn
    n = 2 * seq
    return pl.pallas_call(
        _idft_gate_kernel,
        grid=(batch, nb, mb),
        in_specs=[
            pl.BlockSpec((tm, n), lambda b, j, m: (m, 0)),
            pl.BlockSpec((1, n, tn), lambda b, j, m: (b, 0, j)),
            pl.BlockSpec((tm, tn), lambda b, j, m: (b * mb + m, u_col * nb + j)),
            pl.BlockSpec((tm, tn), lambda b, j, m: (b * mb + m, gate_col * nb + j)),
            pl.BlockSpec((1, tn), lambda b, j, m: (0, j)),
        ],
        out_specs=pl.BlockSpec((tm, tn), lambda b, j, m: (b * mb + m, j)),
        out_shape=jax.ShapeDtypeStruct((batch * seq, d), BF16),
        compiler_params=_params("parallel", "parallel", "arbitrary"),
    )(inv, y.reshape(batch, n, d), u, gate, skip)


def kernel(x, norm_mix_g, norm_ffn_g, a_w_in, a_conv_w, a_w_out, b_w_qkv, b_q_norm_g, b_k_norm_g,
           b_rpb, b_w_out, c_w_in, c_short_w, c_short_b, c_f_w1, c_f_b1, c_f_w2, c_f_b2, c_f_w3,
           c_f_freq, c_f_skip, c_w_out, f_w13, f_w2):
    batch, seq, d = x.shape
    depth = norm_mix_g.shape[0]
    heads = d // NA_HEAD_DIM
    h = x.reshape(batch * seq, d)
    bf = lambda w: w.astype(BF16)
    ia = ib = ic = 0
    for i in range(depth):
        g_mix = norm_mix_g[i][None, :]
        kind = i % 3
        if kind == 0:
            p = _norm_matmul(h, g_mix, bf(a_w_in[ia]))
            h = _sconv_out(p, h, a_conv_w[ia], bf(a_w_out[ia]), seq)
            ia += 1
        elif kind == 1:
            head_gain = jnp.concatenate([
                jnp.tile(b_q_norm_g[ib], heads) * (NA_HEAD_DIM ** -0.5),
                jnp.tile(b_k_norm_g[ib], heads)])[None, :]
            qkv = _norm_matmul(h, g_mix, bf(b_w_qkv[ib]), head_gain)
            att = _natten(qkv.reshape(batch, seq, 3 * d), _bias_table(b_rpb[ib]))
            h = _matmul_residual(att.reshape(batch * seq, d), bf(b_w_out[ib]), h)
            ib += 1
        else:
            fwd, inv = _dft_matrices(seq)
            filt = _hyena_filters_time(seq, d, c_f_w1[ic], c_f_b1[ic], c_f_w2[ic], c_f_b2[ic],
                                       c_f_w3[ic], c_f_freq[ic])
            spec = _matmul(fwd, filt, F32)
            p = _norm_matmul(h, g_mix, bf(c_w_in[ic]))
            vxx = _conv3_bias(p, c_short_w[ic], c_short_b[ic][None, :], seq)
            y = _dft_filter(fwd, vxx, 0, spec, 0, batch, seq, d)
            z = _idft_gate(inv, y, vxx, 0, vxx, 1, c_f_skip[ic][0:1], batch, seq, d)
            y = _dft_filter(fwd, z, 0, spec, 1, batch, seq, d)
            z = _idft_gate(inv, y, z, 0, vxx, 2, c_f_skip[ic][1:2], batch, seq, d)
            h = _matmul_residual(z, bf(c_w_out[ic]), h)
            ic += 1
        h = _ffn(h, norm_ffn_g[i][None, :], bf(f_w13[i]), bf(f_w2[i]))
    return h.reshape(batch, seq, d)
```

```python
import functools
import math

import jax
import jax.numpy as jnp
from jax import lax
from jax.experimental import pallas as pl
from jax.experimental.pallas import tpu as pltpu

F32 = jnp.float32
BF16 = jnp.bfloat16
HIGHEST = lax.Precision.HIGHEST

RMS_EPS = 1e-6
GRID_W = 64
NA_HEAD_DIM = 64
NA_WIN_ROWS = 8
NA_WIN_COLS = 16
NA_RPB_ROWS = 2 * NA_WIN_ROWS - 1
NA_RPB_COLS = 2 * NA_WIN_COLS - 1
HYENA_BANDS = 16
HYENA_DECAY_TARGET = 1e-2
HYENA_FAST_DECAY = 0.3
HYENA_SLOW_DECAY = 1.5
MASK_VALUE = -1e30

V7X_LANES = 128
V7X_SUBLANES = 8
V7X_MXU_DIM = 256
VMEM_LIMIT_BYTES = 56 * 1024 * 1024


def _params(*semantics):
    return pltpu.CompilerParams(dimension_semantics=semantics,
                                vmem_limit_bytes=VMEM_LIMIT_BYTES)


def _rmsnorm_bf16(x, g):
    ms = jnp.mean(x * x, axis=-1, keepdims=True)
    return (x * lax.rsqrt(ms + RMS_EPS) * g).astype(BF16)


def _dot(a, b):
    return jnp.dot(a, b, preferred_element_type=F32)


def _qkv_kernel(x_ref, g_ref, w_ref, hg_ref, o_ref):
    xn = _rmsnorm_bf16(x_ref[...], g_ref[...])
    nc = V7X_MXU_DIM
    qk_cols = hg_ref.shape[1]
    r = lax.broadcasted_iota(jnp.int32, (nc, nc), 0) // NA_HEAD_DIM
    c = lax.broadcasted_iota(jnp.int32, (nc, nc), 1) // NA_HEAD_DIM
    head_ones = jnp.where(r == c, 1.0, 0.0).astype(BF16)
    chunks = list(range(0, w_ref.shape[1], nc))

    def head_sumsq(n, acc):
        return _dot((acc * acc).astype(BF16), head_ones) if n < qk_cols else None

    def finish(n, acc, ssq):
        if ssq is not None:
            acc = acc * lax.rsqrt(ssq * (1.0 / NA_HEAD_DIM) + RMS_EPS) * hg_ref[:, n:n + nc]
        o_ref[:, n:n + nc] = acc.astype(o_ref.dtype)

    acc_q, ssq_q = {}, {}
    for t in range(len(chunks) + 2):
        if t < len(chunks):
            acc_q[t] = _dot(xn, w_ref[:, chunks[t]:chunks[t] + nc])
        if 1 <= t <= len(chunks):
            ssq_q[t - 1] = head_sumsq(chunks[t - 1], acc_q[t - 1])
        if t >= 2:
            finish(chunks[t - 2], acc_q.pop(t - 2), ssq_q.pop(t - 2))


def _qkv(x, g, w, head_gain, *, tm=512):
    t, d = x.shape
    n = w.shape[1]
    return pl.pallas_call(
        _qkv_kernel,
        grid=(t // tm,),
        in_specs=[
            pl.BlockSpec((tm, d), lambda i: (i, 0)),
            pl.BlockSpec((1, d), lambda i: (0, 0)),
            pl.BlockSpec((d, n), lambda i: (0, 0)),
            pl.BlockSpec(head_gain.shape, lambda i: (0, 0)),
        ],
        out_specs=pl.BlockSpec((tm, n), lambda i: (i, 0)),
        out_shape=jax.ShapeDtypeStruct((t, n), BF16),
        compiler_params=_params("parallel"),
    )(x, g, w, head_gain)


def _matmul_residual_kernel(a_ref, w_ref, x_ref, o_ref):
    o_ref[...] = x_ref[...] + _dot(a_ref[...], w_ref[...])


def _matmul_residual(a, w, x, *, tm=512):
    t, k = a.shape
    n = w.shape[1]
    return pl.pallas_call(
        _matmul_residual_kernel,
        grid=(t // tm,),
        in_specs=[
            pl.BlockSpec((tm, k), lambda i: (i, 0)),
            pl.BlockSpec((k, n), lambda i: (0, 0)),
            pl.BlockSpec((tm, n), lambda i: (i, 0)),
        ],
        out_specs=pl.BlockSpec((tm, n), lambda i: (i, 0)),
        out_shape=jax.ShapeDtypeStruct((t, n), F32),
        compiler_params=_params("parallel"),
    )(a, w, x)


def _ffn_kernel(x_ref, g_ref, w13_ref, w2_ref, o_ref, acc_ref, *, hc):
    x = x_ref[...]
    xn = _rmsnorm_bf16(x, g_ref[...])
    hidden = w2_ref.shape[0]
    for c in range(0, hidden, hc):
        gate = _dot(xn, w13_ref[:, c:c + hc])
        up = _dot(xn, w13_ref[:, hidden + c:hidden + c + hc])
        act = (gate * jax.nn.sigmoid(gate) * up).astype(BF16)
        part = _dot(act, w2_ref[c:c + hc, :])
        if c == 0:
            acc_ref[...] = part
        else:
            acc_ref[...] += part
    o_ref[...] = x + acc_ref[...]


def _ffn(x, g, w13, w2, *, tm=512, hc=256):
    t, d = x.shape
    hidden = w2.shape[0]
    return pl.pallas_call(
        functools.partial(_ffn_kernel, hc=hc),
        grid=(t // tm,),
        in_specs=[
            pl.BlockSpec((tm, d), lambda i: (i, 0)),
            pl.BlockSpec((1, d), lambda i: (0, 0)),
            pl.BlockSpec((d, 2 * hidden), lambda i: (0, 0)),
            pl.BlockSpec((hidden, d), lambda i: (0, 0)),
        ],
        out_specs=pl.BlockSpec((tm, d), lambda i: (i, 0)),
        out_shape=jax.ShapeDtypeStruct((t, d), F32),
        scratch_shapes=[pltpu.VMEM((tm, d), F32)],
        compiler_params=_params("parallel"),
    )(x, g, w13, w2)


def _shift_rows(v, prev_row, next_row):
    tm, n = v.shape
    sub = V7X_SUBLANES
    groups = tm // sub
    v3 = v.reshape(groups, sub, n)
    row = lax.broadcasted_iota(jnp.int32, (1, sub, 1), 1)
    down = pltpu.roll(v3, 1, axis=1)
    up = pltpu.roll(v3, sub - 1, axis=1)
    halo = lambda r: jnp.broadcast_to(r[None], (1, sub, n))
    down_nb = jnp.concatenate([halo(prev_row), down[:groups - 1]], axis=0)
    up_nb = jnp.concatenate([up[1:], halo(next_row)], axis=0)
    v_prev = jnp.where(row == 0, down_nb, down)
    v_next = jnp.where(row == sub - 1, up_nb, up)
    return v_prev.reshape(tm, n), v_next.reshape(tm, n)


def _conv3(v, prev_row, next_row, w):
    v_prev, v_next = _shift_rows(v, prev_row, next_row)
    return w[0:1] * v_prev + w[1:2] * v + w[2:3] * v_next


def _finished_tile_edges(i, tiles_per_seq):
    pos = (i - 1) % tiles_per_seq
    return pos == 0, pos == tiles_per_seq - 1


def _delayed_specs(tm, n_tiles):
    current = lambda i: (jnp.minimum(i, n_tiles - 1), 0)
    finished = lambda i: (jnp.maximum(i - 1, 0), 0)
    const = lambda i: (0, 0)
    return current, finished, const


def _sconv_mixer_kernel(x_ref, xf_ref, g_ref, win_ref, cw_ref, wo_ref, o_ref, b_scr, v_scr, edge_scr,
                        *, tiles_per_seq):
    i = pl.program_id(0)
    tm, d = x_ref.shape

    @pl.when(i == 0)
    def _init():
        b_scr[...] = jnp.zeros_like(b_scr)
        v_scr[...] = jnp.zeros_like(v_scr)
        edge_scr[...] = jnp.zeros_like(edge_scr)

    xn = _rmsnorm_bf16(x_ref[...], g_ref[...])
    b_new = _dot(xn, win_ref[:, 0:d]).astype(BF16)
    v_new = _dot(xn, win_ref[:, d:2 * d]) * _dot(xn, win_ref[:, 2 * d:3 * d])

    first, last = _finished_tile_edges(i, tiles_per_seq)
    v = v_scr[...]
    prev_row = jnp.where(first, 0.0, edge_scr[...])
    next_row = jnp.where(last, 0.0, v_new[0:1, :])
    conv = _conv3(v, prev_row, next_row, cw_ref[...])
    y = (b_scr[...].astype(F32) * conv).astype(BF16)
    o_ref[...] = xf_ref[...] + _dot(y, wo_ref[...])

    edge_scr[...] = v[tm - 1:tm, :]
    b_scr[...] = b_new
    v_scr[...] = v_new


def _sconv_mixer(x, g, w_in, conv_w, w_out, seq, *, tm=512):
    t, d = x.shape
    n_tiles = t // tm
    current, finished, const = _delayed_specs(tm, n_tiles)
    return pl.pallas_call(
        functools.partial(_sconv_mixer_kernel, tiles_per_seq=seq // tm),
        grid=(n_tiles + 1,),
        in_specs=[
            pl.BlockSpec((tm, d), current),
            pl.BlockSpec((tm, d), finished),
            pl.BlockSpec((1, d), const),
            pl.BlockSpec((d, 3 * d), const),
            pl.BlockSpec((3, d), const),
            pl.BlockSpec((d, d), const),
        ],
        out_specs=pl.BlockSpec((tm, d), finished),
        out_shape=jax.ShapeDtypeStruct((t, d), F32),
        scratch_shapes=[pltpu.VMEM((tm, d), BF16),
                        pltpu.VMEM((tm, d), F32),
                        pltpu.VMEM((1, d), F32)],
        compiler_params=_params("arbitrary"),
    )(x, x, g, w_in, conv_w, w_out)


def _proj_conv_kernel(x_ref, g_ref, w_ref, cw_ref, cb_ref, o_ref, p_scr, edge_scr, *, tiles_per_seq, nc):
    i = pl.program_id(0)
    tm = x_ref.shape[0]

    @pl.when(i == 0)
    def _init():
        p_scr[...] = jnp.zeros_like(p_scr)
        edge_scr[...] = jnp.zeros_like(edge_scr)

    xn = _rmsnorm_bf16(x_ref[...], g_ref[...])
    first, last = _finished_tile_edges(i, tiles_per_seq)
    for c in range(0, w_ref.shape[1], nc):
        cols = slice(c, c + nc)
        p_new = _dot(xn, w_ref[:, cols])
        p = p_scr[:, cols]
        prev_row = jnp.where(first, 0.0, edge_scr[:, cols])
        next_row = jnp.where(last, 0.0, p_new[0:1, :])
        conv = _conv3(p, prev_row, next_row, cw_ref[:, cols])
        o_ref[:, cols] = (conv + cb_ref[:, cols]).astype(o_ref.dtype)
        edge_scr[:, cols] = p[tm - 1:tm, :]
        p_scr[:, cols] = p_new


def _proj_conv(x, g, w, conv_w, conv_b, seq, *, tm=512, nc=512):
    t, d = x.shape
    n = w.shape[1]
    n_tiles = t // tm
    current, finished, const = _delayed_specs(tm, n_tiles)
    return pl.pallas_call(
        functools.partial(_proj_conv_kernel, tiles_per_seq=seq // tm, nc=nc),
        grid=(n_tiles + 1,),
        in_specs=[
            pl.BlockSpec((tm, d), current),
            pl.BlockSpec((1, d), const),
            pl.BlockSpec((d, n), const),
            pl.BlockSpec((3, n), const),
            pl.BlockSpec((1, n), const),
        ],
        out_specs=pl.BlockSpec((tm, n), finished),
        out_shape=jax.ShapeDtypeStruct((t, n), BF16),
        scratch_shapes=[pltpu.VMEM((tm, n), F32),
                        pltpu.VMEM((1, n), F32)],
        compiler_params=_params("arbitrary"),
    )(x, g, w, conv_w, conv_b)


def _bias_table_kernel(r_ref, o_ref):
    kdim = r_ref.shape[1]
    n = o_ref.shape[1]

    def decode(shape):
        c = lax.broadcasted_iota(jnp.int32, shape, 1)
        qc = c // (2 * GRID_W)
        lane = c % (2 * GRID_W)
        second = lane >= GRID_W
        kc = jnp.where(second, lane - GRID_W, lane)
        start = jnp.clip(qc - NA_WIN_COLS // 2, 0, GRID_W - NA_WIN_COLS)
        valid = (kc >= start) & (kc < start + NA_WIN_COLS)
        idx = kc - qc + (NA_WIN_COLS - 1) + jnp.where(second, NA_RPB_COLS, 0)
        return valid, idx

    valid, idx = decode((kdim, n))
    i = lax.broadcasted_iota(jnp.int32, (kdim, n), 0)
    onehot = jnp.where(valid & (i == idx), 1.0, 0.0)
    tab = jnp.dot(r_ref[...], onehot, preferred_element_type=F32, precision=HIGHEST)
    valid_row, _ = decode((1, n))
    o_ref[...] = jnp.where(valid_row, tab, MASK_VALUE)


def _bias_table(rpb):
    heads = rpb.shape[0]
    pairs = NA_RPB_ROWS - 1
    kdim = 64
    pad = jnp.zeros((pairs, heads, kdim - 2 * NA_RPB_COLS), F32)
    rows = jnp.swapaxes(rpb, 0, 1)
    r = jnp.concatenate([rows[:-1], rows[1:], pad], axis=-1).reshape(pairs * heads, kdim)
    n = GRID_W * 2 * GRID_W
    tab = pl.pallas_call(
        _bias_table_kernel,
        out_shape=jax.ShapeDtypeStruct((pairs * heads, n), F32),
        compiler_params=pltpu.CompilerParams(vmem_limit_bytes=VMEM_LIMIT_BYTES),
    )(r)
    return tab.reshape(pairs, heads * GRID_W, 2 * GRID_W)


def _natten_kernel(q_ref, k_ref, v_ref, tab_ref, o_ref, *, rows_per_step, n_rows):
    pairs = q_ref.shape[2] // V7X_LANES
    lane = lax.broadcasted_iota(jnp.int32, (GRID_W, V7X_LANES), 1)
    low = lane < NA_HEAD_DIM
    win = NA_WIN_ROWS * GRID_W

    def row_body(rr, carry):
        r = pl.program_id(1) * rows_per_step + rr
        rs = jnp.clip(r - NA_WIN_ROWS // 2, 0, n_rows - NA_WIN_ROWS)
        d0 = rs - r + NA_WIN_ROWS - 1
        q_rows = pl.ds(pl.multiple_of(rr * GRID_W, GRID_W), GRID_W)
        k_rows = pl.ds(pl.multiple_of(rs * GRID_W, GRID_W), win)

        def scores(hp):
            cols = slice(hp * V7X_LANES, (hp + 1) * V7X_LANES)
            q2 = q_ref[0, q_rows, cols]
            zero = jnp.zeros_like(q2)
            qs = jnp.concatenate([jnp.where(low, q2, zero), jnp.where(low, zero, q2)], axis=0)
            return lax.dot_general(qs, k_ref[0, k_rows, cols], (((1,), (1,)), ((), ())),
                                   preferred_element_type=F32)

        def softmax(hp, s):
            rows = slice(hp * 2 * GRID_W, (hp + 1) * 2 * GRID_W)
            bias = jnp.concatenate(
                [tab_ref[pl.ds(d0 + 2 * p, 1), rows, :][0] for p in range(NA_WIN_ROWS // 2)], axis=1)
            s = s + bias
            m = jnp.max(s, axis=-1, keepdims=True)
            e = jnp.exp(s - m)
            return e.astype(BF16), 1.0 / jnp.sum(e, axis=-1, keepdims=True)

        def values(hp, p, linv):
            cols = slice(hp * V7X_LANES, (hp + 1) * V7X_LANES)
            o = _dot(p, v_ref[0, k_rows, cols]) * linv
            o_ref[0, q_rows, cols] = jnp.where(low, o[:GRID_W], o[GRID_W:]).astype(o_ref.dtype)

        s_q, p_q = {}, {}
        for t in range(pairs + 2):
            if t < pairs:
                s_q[t] = scores(t)
            if 1 <= t <= pairs:
                p_q[t - 1] = softmax(t - 1, s_q.pop(t - 1))
            if t >= 2:
                values(t - 2, *p_q.pop(t - 2))
        return carry

    lax.fori_loop(0, rows_per_step, row_body, 0)


def _natten(qkv, tab, *, rows_per_step=8):
    b, s, d3 = qkv.shape
    d = d3 // 3
    n_rows = s // GRID_W
    qs = rows_per_step * GRID_W
    return pl.pallas_call(
        functools.partial(_natten_kernel, rows_per_step=rows_per_step, n_rows=n_rows),
        grid=(b, n_rows // rows_per_step),
        in_specs=[
            pl.BlockSpec((1, qs, d), lambda i, j: (i, j, 0)),
            pl.BlockSpec((1, s, d), lambda i, j: (i, 0, 1)),
            pl.BlockSpec((1, s, d), lambda i, j: (i, 0, 2)),
            pl.BlockSpec(tab.shape, lambda i, j: (0, 0, 0)),
        ],
        out_specs=pl.BlockSpec((1, qs, d), lambda i, j: (i, j, 0)),
        out_shape=jax.ShapeDtypeStruct((b, s, d), BF16),
        compiler_params=_params("parallel", "arbitrary"),
    )(qkv, qkv, qkv, tab)


def _dft_matrices(seq):
    n = 2 * seq
    root = int(math.isqrt(n))
    row = jnp.arange(n, dtype=jnp.int32)
    keff = jnp.where(row <= seq, row, row - seq)
    quarter = jnp.where(row > seq, n // 4, 0)
    hi = root * jnp.arange(root, dtype=jnp.int32)
    lo = jnp.arange(root, dtype=jnp.int32)
    step = 2.0 * math.pi / n
    ang_hi = ((keff[:, None] * hi[None, :]) % n).astype(F32) * step
    ang_lo = ((keff[:, None] * lo[None, :] + quarter[:, None]) % n).astype(F32) * step
    c_hi, s_hi, c_lo, s_lo = jnp.cos(ang_hi), jnp.sin(ang_hi), jnp.cos(ang_lo), jnp.sin(ang_lo)
    fwd = c_hi[:, :, None] * c_lo[:, None, :] - s_hi[:, :, None] * s_lo[:, None, :]
    fwd = fwd.reshape(n, n).astype(BF16)
    half = root // 2
    inv = (c_hi.T[:half, None, :] * c_lo.T[None, :, :] - s_hi.T[:half, None, :] * s_lo.T[None, :, :])
    inv = inv.reshape(seq, n).astype(BF16)
    return fwd, inv


def _filter_kernel(bands_ref, w1t_ref, w1c_ref, w1s_ref, b1_ref, w2_ref, b2_ref, freq_ref,
                   w3f_ref, w3r_ref, delta_ref, h_ref, hid_scr):
    seq = h_ref.shape[0] // 2
    j = lax.broadcasted_iota(jnp.int32, (seq, 1), 0)
    pos_f = j.astype(F32)
    pos_r = jnp.where(j == 0, 0, seq - j).astype(F32)

    def mlp(pos):
        t = pos / (seq - 1.0)
        ang = (2.0 * math.pi) * pos / seq * bands_ref[...]
        pre = (t * w1t_ref[...]
               + jnp.dot(jnp.cos(ang), w1c_ref[...], preferred_element_type=F32, precision=HIGHEST)
               + jnp.dot(-jnp.sin(ang), w1s_ref[...], preferred_element_type=F32, precision=HIGHEST)
               + b1_ref[...])
        hid = jnp.sin(freq_ref[...] * pre)
        return jnp.sin(freq_ref[...] * (
            jnp.dot(hid, w2_ref[...], preferred_element_type=F32, precision=HIGHEST) + b2_ref[...]))

    @pl.when((pl.program_id(0) == 0) & (pl.program_id(1) == 0))
    def _hidden():
        hid_scr[0] = mlp(pos_f)
        hid_scr[1] = mlp(pos_r)

    def filt(hid, w3_ref, pos):
        f = jnp.dot(hid, w3_ref[...], preferred_element_type=F32, precision=HIGHEST)
        return f * jnp.exp(-(pos / (seq - 1.0)) * delta_ref[...])

    fwd = filt(hid_scr[0], w3f_ref, pos_f)
    rev = filt(hid_scr[1], w3r_ref, pos_r)
    top = fwd + jnp.where(j == 0, rev, 0.0)
    bot = jnp.where(j == 0, 0.0, rev)
    l1 = jnp.sum(jnp.abs(top), axis=0, keepdims=True) + jnp.sum(jnp.abs(bot), axis=0, keepdims=True)
    inv = 1.0 / l1
    h_ref[0:seq, :] = (top * inv).astype(h_ref.dtype)
    h_ref[seq:2 * seq, :] = (bot * inv).astype(h_ref.dtype)


def _hyena_filters_time(seq, d, w1, b1, w2, b2, w3, freq, *, tn=512):
    nb = HYENA_BANDS
    hid = w2.shape[0]
    bands = jnp.linspace(1e-4, nb - 1, nb, dtype=F32)[None, :]
    lt = math.log(HYENA_DECAY_TARGET)
    deltas = jnp.abs(jnp.linspace(lt / HYENA_SLOW_DECAY, lt / HYENA_FAST_DECAY, d, dtype=F32))[None, :]
    small = lambda shape: pl.BlockSpec(shape, lambda o, c: (0, 0))
    cpo = d // tn
    return pl.pallas_call(
        _filter_kernel,
        grid=(2, cpo),
        in_specs=[
            small((1, nb)), small((1, hid)), small((nb, hid)), small((nb, hid)), small((1, hid)),
            small((hid, hid)), small((1, hid)), small((1, hid)),
            pl.BlockSpec((hid, tn), lambda o, c: (0, 2 * o * cpo + c)),
            pl.BlockSpec((hid, tn), lambda o, c: (0, (2 * o + 1) * cpo + c)),
            pl.BlockSpec((1, tn), lambda o, c: (0, c)),
        ],
        out_specs=pl.BlockSpec((2 * seq, tn), lambda o, c: (0, o * cpo + c)),
        out_shape=jax.ShapeDtypeStruct((2 * seq, 2 * d), BF16),
        scratch_shapes=[pltpu.VMEM((2, seq, hid), F32)],
        compiler_params=_params("arbitrary", "arbitrary"),
    )(bands, w1[0:1], w1[1:1 + nb], w1[1 + nb:1 + 2 * nb], b1[None, :], w2, b2[None, :],
      freq[None, :], w3, w3, deltas)


def _matmul_kernel(a_ref, b_ref, o_ref):
    o_ref[...] = _dot(a_ref[...], b_ref[...]).astype(o_ref.dtype)


def _matmul(a, b, out_dtype, *, tm=512, tn=512):
    m, k = a.shape
    n = b.shape[1]
    return pl.pallas_call(
        _matmul_kernel,
        grid=(m // tm, n // tn),
        in_specs=[pl.BlockSpec((tm, k), lambda i, j: (i, 0)),
                  pl.BlockSpec((k, tn), lambda i, j: (0, j))],
        out_specs=pl.BlockSpec((tm, tn), lambda i, j: (i, j)),
        out_shape=jax.ShapeDtypeStruct((m, n), out_dtype),
        compiler_params=_params("parallel", "parallel"),
    )(a, b)


def _dft_filter_kernel(c_ref, s_ref, u_ref, hre_ref, him_ref, y_ref, *, nc):
    tk = c_ref.shape[0]
    n = 2 * c_ref.shape[1]
    rows = pl.program_id(0) * tk + lax.broadcasted_iota(jnp.int32, (tk, 1), 0)
    packed = rows == 0
    for c in range(0, u_ref.shape[1], nc):
        cols = slice(c, c + nc)
        u = u_ref[:, cols]
        ure = _dot(c_ref[...], u)
        uim = _dot(s_ref[...], u)
        hre = hre_ref[:, cols]
        him = him_ref[:, cols]
        yre = jnp.where(packed, 0.5 * ure * hre, ure * hre - uim * him) * (2.0 / n)
        yim = jnp.where(packed, 0.5 * uim * him, ure * him + uim * hre) * (2.0 / n)
        y_ref[0, 0, :, cols] = yre.astype(y_ref.dtype)
        y_ref[0, 1, :, cols] = yim.astype(y_ref.dtype)


def _dft_filter(fwd, u, spec, order, batch, seq, d, *, tk=512, nc=512):
    kb = seq // tk
    return pl.pallas_call(
        functools.partial(_dft_filter_kernel, nc=nc),
        grid=(kb, batch),
        in_specs=[
            pl.BlockSpec((tk, seq), lambda k, b: (k, 0)),
            pl.BlockSpec((tk, seq), lambda k, b: (kb + k, 0)),
            pl.BlockSpec((seq, d), lambda k, b: (b, 0)),
            pl.BlockSpec((tk, d), lambda k, b: (k, order)),
            pl.BlockSpec((tk, d), lambda k, b: (kb + k, order)),
        ],
        out_specs=pl.BlockSpec((1, 2, tk, d), lambda k, b: (b, 0, k, 0)),
        out_shape=jax.ShapeDtypeStruct((batch, 2, seq, d), BF16),
        compiler_params=_params("parallel", "arbitrary"),
    )(fwd, fwd, u, spec, spec)


def _idft_gate_chunks(g_ref, y_ref, u_ref, gate_ref, skip_ref, nc):
    for c in range(0, u_ref.shape[1], nc):
        cols = slice(c, c + nc)
        y = _dot(g_ref[...], y_ref[0, :, cols])
        u = u_ref[:, cols].astype(F32)
        yield cols, (gate_ref[:, cols].astype(F32) * (y + u * skip_ref[:, cols])).astype(BF16)


def _idft_gate_kernel(g_ref, y_ref, u_ref, gate_ref, skip_ref, o_ref, *, nc):
    for cols, z in _idft_gate_chunks(g_ref, y_ref, u_ref, gate_ref, skip_ref, nc):
        o_ref[:, cols] = z


def _idft_gate_out_kernel(g_ref, y_ref, u_ref, gate_ref, skip_ref, wo_ref, x_ref, o_ref, *, nc):
    z = jnp.concatenate(
        [z for _, z in _idft_gate_chunks(g_ref, y_ref, u_ref, gate_ref, skip_ref, nc)], axis=1)
    o_ref[...] = x_ref[...] + _dot(z, wo_ref[...])


def _idft_gate(inv, y, u, gate, gate_col, skip, batch, seq, d, w_out=None, x=None, *, tm=512, nc=512):
    mb = seq // tm
    n = 2 * seq
    rows = lambda b, m: (b * mb + m, 0)
    in_specs = [
        pl.BlockSpec((tm, n), lambda b, m: (m, 0)),
        pl.BlockSpec((1, n, d), lambda b, m: (b, 0, 0)),
        pl.BlockSpec((tm, d), rows),
        pl.BlockSpec((tm, d), lambda b, m: (b * mb + m, gate_col)),
        pl.BlockSpec((1, d), lambda b, m: (0, 0)),
    ]
    args = [inv, y.reshape(batch, n, d), u, gate, skip]
    if w_out is None:
        body, out_dtype = _idft_gate_kernel, BF16
    else:
        body, out_dtype = _idft_gate_out_kernel, F32
        in_specs += [pl.BlockSpec((d, d), lambda b, m: (0, 0)), pl.BlockSpec((tm, d), rows)]
        args += [w_out, x]
    return pl.pallas_call(
        functools.partial(body, nc=nc),
        grid=(batch, mb),
        in_specs=in_specs,
        out_specs=pl.BlockSpec((tm, d), rows),
        out_shape=jax.ShapeDtypeStruct((batch * seq, d), out_dtype),
        compiler_params=_params("parallel", "arbitrary"),
    )(*args)


def kernel(x, norm_mix_g, norm_ffn_g, a_w_in, a_conv_w, a_w_out, b_w_qkv, b_q_norm_g, b_k_norm_g,
           b_rpb, b_w_out, c_w_in, c_short_w, c_short_b, c_f_w1, c_f_b1, c_f_w2, c_f_b2, c_f_w3,
           c_f_freq, c_f_skip, c_w_out, f_w13, f_w2):
    batch, seq, d = x.shape
    depth = norm_mix_g.shape[0]
    heads = d // NA_HEAD_DIM
    h = x.reshape(batch * seq, d)
    bf = lambda w: w.astype(BF16)
    ia = ib = ic = 0
    for i in range(depth):
        g_mix = norm_mix_g[i][None, :]
        kind = i % 3
        if kind == 0:
            h = _sconv_mixer(h, g_mix, bf(a_w_in[ia]), a_conv_w[ia], bf(a_w_out[ia]), seq)
            ia += 1
        elif kind == 1:
            head_gain = jnp.concatenate([
                jnp.tile(b_q_norm_g[ib], heads) * (NA_HEAD_DIM ** -0.5),
                jnp.tile(b_k_norm_g[ib], heads)])[None, :]
            qkv = _qkv(h, g_mix, bf(b_w_qkv[ib]), head_gain)
            att = _natten(qkv.reshape(batch, seq, 3 * d), _bias_table(b_rpb[ib]))
            h = _matmul_residual(att.reshape(batch * seq, d), bf(b_w_out[ib]), h)
            ib += 1
        else:
            fwd, inv = _dft_matrices(seq)
            filt = _hyena_filters_time(seq, d, c_f_w1[ic], c_f_b1[ic], c_f_w2[ic], c_f_b2[ic],
                                       c_f_w3[ic], c_f_freq[ic])
            spec = _matmul(fwd, filt, F32)
            vxx = _proj_conv(h, g_mix, bf(c_w_in[ic]), c_short_w[ic], c_short_b[ic][None, :], seq)
            skip = c_f_skip[ic]
            y = _dft_filter(fwd, vxx, spec, 0, batch, seq, d)
            z = _idft_gate(inv, y, vxx, vxx, 1, skip[0:1], batch, seq, d)
            y = _dft_filter(fwd, z, spec, 1, batch, seq, d)
            h = _idft_gate(inv, y, z, vxx, 2, skip[1:2], batch, seq, d, bf(c_w_out[ic]), h)
            ic += 1
        h = _ffn(h, norm_ffn_g[i][None, :], bf(f_w13[i]), bf(f_w2[i]))
    return h.reshape(batch, seq, d)
```

```python
import functools
import math

import jax
import jax.numpy as jnp
from jax import lax
from jax.experimental import pallas as pl
from jax.experimental.pallas import tpu as pltpu

F32 = jnp.float32
BF16 = jnp.bfloat16
HIGHEST = lax.Precision.HIGHEST

RMS_EPS = 1e-6
GRID_W = 64
NA_HEAD_DIM = 64
NA_WIN_ROWS = 8
NA_WIN_COLS = 16
NA_RPB_ROWS = 2 * NA_WIN_ROWS - 1
NA_RPB_COLS = 2 * NA_WIN_COLS - 1
HYENA_BANDS = 16
HYENA_DECAY_TARGET = 1e-2
HYENA_FAST_DECAY = 0.3
HYENA_SLOW_DECAY = 1.5
HYENA_CONV_BLOCKS = 4
MASK_VALUE = -1e30

V7X_LANES = 128
V7X_SUBLANES = 8
V7X_MXU_DIM = 256
VMEM_LIMIT_BYTES = 56 * 1024 * 1024


def _params(*semantics):
    return pltpu.CompilerParams(dimension_semantics=semantics,
                                vmem_limit_bytes=VMEM_LIMIT_BYTES)


def _rmsnorm_bf16(x, g):
    ms = jnp.mean(x * x, axis=-1, keepdims=True)
    return (x * lax.rsqrt(ms + RMS_EPS) * g).astype(BF16)


def _dot(a, b):
    return jnp.dot(a, b, preferred_element_type=F32)


def _qkv_kernel(x_ref, g_ref, w_ref, hg_ref, o_ref):
    xn = _rmsnorm_bf16(x_ref[...], g_ref[...])
    nc = V7X_MXU_DIM
    qk_cols = hg_ref.shape[1]
    r = lax.broadcasted_iota(jnp.int32, (nc, nc), 0) // NA_HEAD_DIM
    c = lax.broadcasted_iota(jnp.int32, (nc, nc), 1) // NA_HEAD_DIM
    head_ones = jnp.where(r == c, 1.0, 0.0).astype(BF16)
    chunks = list(range(0, w_ref.shape[1], nc))

    def head_sumsq(n, acc):
        return _dot((acc * acc).astype(BF16), head_ones) if n < qk_cols else None

    def finish(n, acc, ssq):
        if ssq is not None:
            acc = acc * lax.rsqrt(ssq * (1.0 / NA_HEAD_DIM) + RMS_EPS) * hg_ref[:, n:n + nc]
        o_ref[:, n:n + nc] = acc.astype(o_ref.dtype)

    acc_q, ssq_q = {}, {}
    for t in range(len(chunks) + 2):
        if t < len(chunks):
            acc_q[t] = _dot(xn, w_ref[:, chunks[t]:chunks[t] + nc])
        if 1 <= t <= len(chunks):
            ssq_q[t - 1] = head_sumsq(chunks[t - 1], acc_q[t - 1])
        if t >= 2:
            finish(chunks[t - 2], acc_q.pop(t - 2), ssq_q.pop(t - 2))


def _qkv(x, g, w, head_gain, *, tm=512):
    t, d = x.shape
    n = w.shape[1]
    return pl.pallas_call(
        _qkv_kernel,
        grid=(t // tm,),
        in_specs=[
            pl.BlockSpec((tm, d), lambda i: (i, 0)),
            pl.BlockSpec((1, d), lambda i: (0, 0)),
            pl.BlockSpec((d, n), lambda i: (0, 0)),
            pl.BlockSpec(head_gain.shape, lambda i: (0, 0)),
        ],
        out_specs=pl.BlockSpec((tm, n), lambda i: (i, 0)),
        out_shape=jax.ShapeDtypeStruct((t, n), BF16),
        compiler_params=_params("parallel"),
    )(x, g, w, head_gain)


def _matmul_residual_kernel(a_ref, w_ref, x_ref, o_ref):
    o_ref[...] = x_ref[...] + _dot(a_ref[...], w_ref[...])


def _matmul_residual(a, w, x, *, tm=512):
    t, k = a.shape
    n = w.shape[1]
    return pl.pallas_call(
        _matmul_residual_kernel,
        grid=(t // tm,),
        in_specs=[
            pl.BlockSpec((tm, k), lambda i: (i, 0)),
            pl.BlockSpec((k, n), lambda i: (0, 0)),
            pl.BlockSpec((tm, n), lambda i: (i, 0)),
        ],
        out_specs=pl.BlockSpec((tm, n), lambda i: (i, 0)),
        out_shape=jax.ShapeDtypeStruct((t, n), F32),
        compiler_params=_params("parallel"),
    )(a, w, x)


def _ffn_kernel(x_ref, g_ref, w13_ref, w2_ref, o_ref, acc_ref, *, hc):
    x = x_ref[...]
    xn = _rmsnorm_bf16(x, g_ref[...])
    hidden = w2_ref.shape[0]
    for c in range(0, hidden, hc):
        gate = _dot(xn, w13_ref[:, c:c + hc])
        up = _dot(xn, w13_ref[:, hidden + c:hidden + c + hc])
        act = (gate * jax.nn.sigmoid(gate) * up).astype(BF16)
        part = _dot(act, w2_ref[c:c + hc, :])
        if c == 0:
            acc_ref[...] = part
        else:
            acc_ref[...] += part
    o_ref[...] = x + acc_ref[...]


def _ffn(x, g, w13, w2, *, tm=512, hc=256):
    t, d = x.shape
    hidden = w2.shape[0]
    return pl.pallas_call(
        functools.partial(_ffn_kernel, hc=hc),
        grid=(t // tm,),
        in_specs=[
            pl.BlockSpec((tm, d), lambda i: (i, 0)),
            pl.BlockSpec((1, d), lambda i: (0, 0)),
            pl.BlockSpec((d, 2 * hidden), lambda i: (0, 0)),
            pl.BlockSpec((hidden, d), lambda i: (0, 0)),
        ],
        out_specs=pl.BlockSpec((tm, d), lambda i: (i, 0)),
        out_shape=jax.ShapeDtypeStruct((t, d), F32),
        scratch_shapes=[pltpu.VMEM((tm, d), F32)],
        compiler_params=_params("parallel"),
    )(x, g, w13, w2)


def _shift_rows(v, prev_row, next_row):
    tm, n = v.shape
    sub = V7X_SUBLANES
    groups = tm // sub
    v3 = v.reshape(groups, sub, n)
    row = lax.broadcasted_iota(jnp.int32, (1, sub, 1), 1)
    down = pltpu.roll(v3, 1, axis=1)
    up = pltpu.roll(v3, sub - 1, axis=1)
    halo = lambda r: jnp.broadcast_to(r[None], (1, sub, n))
    down_nb = jnp.concatenate([halo(prev_row), down[:groups - 1]], axis=0)
    up_nb = jnp.concatenate([up[1:], halo(next_row)], axis=0)
    v_prev = jnp.where(row == 0, down_nb, down)
    v_next = jnp.where(row == sub - 1, up_nb, up)
    return v_prev.reshape(tm, n), v_next.reshape(tm, n)


def _conv3(v, prev_row, next_row, w):
    v_prev, v_next = _shift_rows(v, prev_row, next_row)
    return w[0:1] * v_prev + w[1:2] * v + w[2:3] * v_next


def _finished_tile_edges(i, tiles_per_seq):
    pos = (i - 1) % tiles_per_seq
    return pos == 0, pos == tiles_per_seq - 1


def _delayed_specs(tm, n_tiles):
    current = lambda i: (jnp.minimum(i, n_tiles - 1), 0)
    finished = lambda i: (jnp.maximum(i - 1, 0), 0)
    const = lambda i: (0, 0)
    return current, finished, const


def _sconv_mixer_kernel(x_ref, xf_ref, g_ref, win_ref, cw_ref, wo_ref, o_ref, b_scr, v_scr, edge_scr,
                        *, tiles_per_seq):
    i = pl.program_id(0)
    tm, d = x_ref.shape

    @pl.when(i == 0)
    def _init():
        b_scr[...] = jnp.zeros_like(b_scr)
        v_scr[...] = jnp.zeros_like(v_scr)
        edge_scr[...] = jnp.zeros_like(edge_scr)

    xn = _rmsnorm_bf16(x_ref[...], g_ref[...])
    b_new = _dot(xn, win_ref[:, 0:d]).astype(BF16)
    v_new = _dot(xn, win_ref[:, d:2 * d]) * _dot(xn, win_ref[:, 2 * d:3 * d])

    first, last = _finished_tile_edges(i, tiles_per_seq)
    v = v_scr[...]
    prev_row = jnp.where(first, 0.0, edge_scr[...])
    next_row = jnp.where(last, 0.0, v_new[0:1, :])
    conv = _conv3(v, prev_row, next_row, cw_ref[...])
    y = (b_scr[...].astype(F32) * conv).astype(BF16)
    o_ref[...] = xf_ref[...] + _dot(y, wo_ref[...])

    edge_scr[...] = v[tm - 1:tm, :]
    b_scr[...] = b_new
    v_scr[...] = v_new


def _sconv_mixer(x, g, w_in, conv_w, w_out, seq, *, tm=512):
    t, d = x.shape
    n_tiles = t // tm
    current, finished, const = _delayed_specs(tm, n_tiles)
    return pl.pallas_call(
        functools.partial(_sconv_mixer_kernel, tiles_per_seq=seq // tm),
        grid=(n_tiles + 1,),
        in_specs=[
            pl.BlockSpec((tm, d), current),
            pl.BlockSpec((tm, d), finished),
            pl.BlockSpec((1, d), const),
            pl.BlockSpec((d, 3 * d), const),
            pl.BlockSpec((3, d), const),
            pl.BlockSpec((d, d), const),
        ],
        out_specs=pl.BlockSpec((tm, d), finished),
        out_shape=jax.ShapeDtypeStruct((t, d), F32),
        scratch_shapes=[pltpu.VMEM((tm, d), BF16),
                        pltpu.VMEM((tm, d), F32),
                        pltpu.VMEM((1, d), F32)],
        compiler_params=_params("arbitrary"),
    )(x, x, g, w_in, conv_w, w_out)


def _proj_conv_kernel(x_ref, g_ref, w_ref, cw_ref, cb_ref, o_ref, p_scr, edge_scr, *, tiles_per_seq, nc):
    i = pl.program_id(0)
    tm = x_ref.shape[0]

    @pl.when(i == 0)
    def _init():
        p_scr[...] = jnp.zeros_like(p_scr)
        edge_scr[...] = jnp.zeros_like(edge_scr)

    xn = _rmsnorm_bf16(x_ref[...], g_ref[...])
    first, last = _finished_tile_edges(i, tiles_per_seq)
    for c in range(0, w_ref.shape[1], nc):
        cols = slice(c, c + nc)
        p_new = _dot(xn, w_ref[:, cols])
        p = p_scr[:, cols]
        prev_row = jnp.where(first, 0.0, edge_scr[:, cols])
        next_row = jnp.where(last, 0.0, p_new[0:1, :])
        conv = _conv3(p, prev_row, next_row, cw_ref[:, cols])
        o_ref[:, cols] = (conv + cb_ref[:, cols]).astype(o_ref.dtype)
        edge_scr[:, cols] = p[tm - 1:tm, :]
        p_scr[:, cols] = p_new


def _proj_conv(x, g, w, conv_w, conv_b, seq, *, tm=512, nc=512):
    t, d = x.shape
    n = w.shape[1]
    n_tiles = t // tm
    current, finished, const = _delayed_specs(tm, n_tiles)
    return pl.pallas_call(
        functools.partial(_proj_conv_kernel, tiles_per_seq=seq // tm, nc=nc),
        grid=(n_tiles + 1,),
        in_specs=[
            pl.BlockSpec((tm, d), current),
            pl.BlockSpec((1, d), const),
            pl.BlockSpec((d, n), const),
            pl.BlockSpec((3, n), const),
            pl.BlockSpec((1, n), const),
        ],
        out_specs=pl.BlockSpec((tm, n), finished),
        out_shape=jax.ShapeDtypeStruct((t, n), BF16),
        scratch_shapes=[pltpu.VMEM((tm, n), F32),
                        pltpu.VMEM((1, n), F32)],
        compiler_params=_params("arbitrary"),
    )(x, g, w, conv_w, conv_b)


def _bias_table_kernel(r_ref, o_ref):
    kdim = r_ref.shape[1]
    n = o_ref.shape[1]

    def decode(shape):
        c = lax.broadcasted_iota(jnp.int32, shape, 1)
        qc = c // (2 * GRID_W)
        lane = c % (2 * GRID_W)
        second = lane >= GRID_W
        kc = jnp.where(second, lane - GRID_W, lane)
        start = jnp.clip(qc - NA_WIN_COLS // 2, 0, GRID_W - NA_WIN_COLS)
        valid = (kc >= start) & (kc < start + NA_WIN_COLS)
        idx = kc - qc + (NA_WIN_COLS - 1) + jnp.where(second, NA_RPB_COLS, 0)
        return valid, idx

    valid, idx = decode((kdim, n))
    i = lax.broadcasted_iota(jnp.int32, (kdim, n), 0)
    onehot = jnp.where(valid & (i == idx), 1.0, 0.0)
    tab = jnp.dot(r_ref[...], onehot, preferred_element_type=F32, precision=HIGHEST)
    valid_row, _ = decode((1, n))
    o_ref[...] = jnp.where(valid_row, tab, MASK_VALUE)


def _bias_table(rpb):
    heads = rpb.shape[0]
    pairs = NA_RPB_ROWS - 1
    kdim = 64
    pad = jnp.zeros((pairs, heads, kdim - 2 * NA_RPB_COLS), F32)
    rows = jnp.swapaxes(rpb, 0, 1)
    r = jnp.concatenate([rows[:-1], rows[1:], pad], axis=-1).reshape(pairs * heads, kdim)
    n = GRID_W * 2 * GRID_W
    tab = pl.pallas_call(
        _bias_table_kernel,
        out_shape=jax.ShapeDtypeStruct((pairs * heads, n), F32),
        compiler_params=pltpu.CompilerParams(vmem_limit_bytes=VMEM_LIMIT_BYTES),
    )(r)
    return tab.reshape(pairs, heads * GRID_W, 2 * GRID_W)


def _natten_kernel(q_ref, k_ref, v_ref, tab_ref, wo_ref, x_ref, o_ref, att_scr, *, rows_per_step, n_rows):
    pairs = q_ref.shape[2] // V7X_LANES
    lane = lax.broadcasted_iota(jnp.int32, (GRID_W, V7X_LANES), 1)
    low = lane < NA_HEAD_DIM
    win = NA_WIN_ROWS * GRID_W

    def row_body(rr, carry):
        r = pl.program_id(1) * rows_per_step + rr
        rs = jnp.clip(r - NA_WIN_ROWS // 2, 0, n_rows - NA_WIN_ROWS)
        d0 = rs - r + NA_WIN_ROWS - 1
        q_rows = pl.ds(pl.multiple_of(rr * GRID_W, GRID_W), GRID_W)
        k_rows = pl.ds(pl.multiple_of(rs * GRID_W, GRID_W), win)

        def scores(hp):
            cols = slice(hp * V7X_LANES, (hp + 1) * V7X_LANES)
            q2 = q_ref[0, q_rows, cols]
            zero = jnp.zeros_like(q2)
            qs = jnp.concatenate([jnp.where(low, q2, zero), jnp.where(low, zero, q2)], axis=0)
            return lax.dot_general(qs, k_ref[0, k_rows, cols], (((1,), (1,)), ((), ())),
                                   preferred_element_type=F32)

        def softmax(hp, s):
            rows = slice(hp * 2 * GRID_W, (hp + 1) * 2 * GRID_W)
            bias = jnp.concatenate(
                [tab_ref[pl.ds(d0 + 2 * p, 1), rows, :][0] for p in range(NA_WIN_ROWS // 2)], axis=1)
            s = s + bias
            m = jnp.max(s, axis=-1, keepdims=True)
            e = jnp.exp(s - m)
            return e.astype(BF16), 1.0 / jnp.sum(e, axis=-1, keepdims=True)

        def values(hp, p, linv):
            cols = slice(hp * V7X_LANES, (hp + 1) * V7X_LANES)
            o = _dot(p, v_ref[0, k_rows, cols]) * linv
            att_scr[q_rows, cols] = jnp.where(low, o[:GRID_W], o[GRID_W:]).astype(att_scr.dtype)

        s_q, p_q = {}, {}
        for t in range(pairs + 2):
            if t < pairs:
                s_q[t] = scores(t)
            if 1 <= t <= pairs:
                p_q[t - 1] = softmax(t - 1, s_q.pop(t - 1))
            if t >= 2:
                values(t - 2, *p_q.pop(t - 2))
        return carry

    lax.fori_loop(0, rows_per_step, row_body, 0)
    o_ref[0] = x_ref[0] + _dot(att_scr[...], wo_ref[...])


def _natten(qkv, tab, w_out, x, *, rows_per_step=8):
    b, s, d3 = qkv.shape
    d = d3 // 3
    n_rows = s // GRID_W
    qs = rows_per_step * GRID_W
    return pl.pallas_call(
        functools.partial(_natten_kernel, rows_per_step=rows_per_step, n_rows=n_rows),
        grid=(b, n_rows // rows_per_step),
        in_specs=[
            pl.BlockSpec((1, qs, d), lambda i, j: (i, j, 0)),
            pl.BlockSpec((1, s, d), lambda i, j: (i, 0, 1)),
            pl.BlockSpec((1, s, d), lambda i, j: (i, 0, 2)),
            pl.BlockSpec(tab.shape, lambda i, j: (0, 0, 0)),
            pl.BlockSpec((d, d), lambda i, j: (0, 0)),
            pl.BlockSpec((1, qs, d), lambda i, j: (i, j, 0)),
        ],
        out_specs=pl.BlockSpec((1, qs, d), lambda i, j: (i, j, 0)),
        out_shape=jax.ShapeDtypeStruct((b, s, d), F32),
        scratch_shapes=[pltpu.VMEM((qs, d), BF16)],
        compiler_params=_params("parallel", "arbitrary"),
    )(qkv, qkv, qkv, tab, w_out, x)


def _block_dft_matrices(bk):
    n = 2 * bk
    row = jnp.arange(n, dtype=jnp.int32)
    keff = jnp.where(row <= bk, row, row - bk)
    quarter = jnp.where(row > bk, n // 4, 0)
    t = jnp.arange(n, dtype=jnp.int32)
    turns = (keff[:, None] * t[None, :] + quarter[:, None]) % n
    full = jnp.cos(turns.astype(F32) * (2.0 * math.pi / n))
    weight = jnp.where((row == 0) | (row == bk), 1.0 / n, 2.0 / n)
    inv = (full[:, :bk] * weight[:, None]).T
    return full[:, :bk].astype(BF16), full[:, bk:].astype(BF16), inv.astype(BF16)


def _filter_kernel(bands_ref, w1t_ref, w1c_ref, w1s_ref, b1_ref, w2_ref, b2_ref, freq_ref,
                   w3f_ref, w3r_ref, delta_ref, h_ref, hid_scr):
    seq = h_ref.shape[0] // 2
    j = lax.broadcasted_iota(jnp.int32, (seq, 1), 0)
    pos_f = j.astype(F32)
    pos_r = jnp.where(j == 0, 0, seq - j).astype(F32)

    def mlp(pos):
        t = pos / (seq - 1.0)
        ang = (2.0 * math.pi) * pos / seq * bands_ref[...]
        pre = (t * w1t_ref[...]
               + jnp.dot(jnp.cos(ang), w1c_ref[...], preferred_element_type=F32, precision=HIGHEST)
               + jnp.dot(-jnp.sin(ang), w1s_ref[...], preferred_element_type=F32, precision=HIGHEST)
               + b1_ref[...])
        hid = jnp.sin(freq_ref[...] * pre)
        return jnp.sin(freq_ref[...] * (
            jnp.dot(hid, w2_ref[...], preferred_element_type=F32, precision=HIGHEST) + b2_ref[...]))

    @pl.when((pl.program_id(0) == 0) & (pl.program_id(1) == 0))
    def _hidden():
        hid_scr[0] = mlp(pos_f)
        hid_scr[1] = mlp(pos_r)

    def filt(hid, w3_ref, pos):
        f = jnp.dot(hid, w3_ref[...], preferred_element_type=F32, precision=HIGHEST)
        return f * jnp.exp(-(pos / (seq - 1.0)) * delta_ref[...])

    fwd = filt(hid_scr[0], w3f_ref, pos_f)
    rev = filt(hid_scr[1], w3r_ref, pos_r)
    top = fwd + jnp.where(j == 0, rev, 0.0)
    bot = jnp.where(j == 0, 0.0, rev)
    l1 = jnp.sum(jnp.abs(top), axis=0, keepdims=True) + jnp.sum(jnp.abs(bot), axis=0, keepdims=True)
    inv = 1.0 / l1
    h_ref[0:seq, :] = (top * inv).astype(h_ref.dtype)
    h_ref[seq:2 * seq, :] = (bot * inv).astype(h_ref.dtype)


def _hyena_filters_time(seq, d, w1, b1, w2, b2, w3, freq, *, tn=512):
    nb = HYENA_BANDS
    hid = w2.shape[0]
    bands = jnp.linspace(1e-4, nb - 1, nb, dtype=F32)[None, :]
    lt = math.log(HYENA_DECAY_TARGET)
    deltas = jnp.abs(jnp.linspace(lt / HYENA_SLOW_DECAY, lt / HYENA_FAST_DECAY, d, dtype=F32))[None, :]
    small = lambda shape: pl.BlockSpec(shape, lambda o, c: (0, 0))
    cpo = d // tn
    return pl.pallas_call(
        _filter_kernel,
        grid=(2, cpo),
        in_specs=[
            small((1, nb)), small((1, hid)), small((nb, hid)), small((nb, hid)), small((1, hid)),
            small((hid, hid)), small((1, hid)), small((1, hid)),
            pl.BlockSpec((hid, tn), lambda o, c: (0, 2 * o * cpo + c)),
            pl.BlockSpec((hid, tn), lambda o, c: (0, (2 * o + 1) * cpo + c)),
            pl.BlockSpec((1, tn), lambda o, c: (0, c)),
        ],
        out_specs=pl.BlockSpec((2 * seq, tn), lambda o, c: (0, o * cpo + c)),
        out_shape=jax.ShapeDtypeStruct((2 * seq, 2 * d), BF16),
        scratch_shapes=[pltpu.VMEM((2, seq, hid), F32)],
        compiler_params=_params("arbitrary", "arbitrary"),
    )(bands, w1[0:1], w1[1:1 + nb], w1[1 + nb:1 + 2 * nb], b1[None, :], w2, b2[None, :],
      freq[None, :], w3, w3, deltas)


def _filter_spectra_kernel(fl_ref, fr_ref, top_ref, bot_ref, o_ref):
    o_ref[0] = _dot(fl_ref[...], top_ref[...]) + _dot(fr_ref[...], bot_ref[...])


def _filter_spectra(fl, fr, filt, *, tn=1024):
    n, bk = fl.shape
    blocks = filt.shape[0] // bk
    nb = blocks // 2
    cols = filt.shape[1]
    half = pl.BlockSpec((n, bk), lambda e, j: (0, 0))
    return pl.pallas_call(
        _filter_spectra_kernel,
        grid=(2 * nb - 1, cols // tn),
        in_specs=[
            half, half,
            pl.BlockSpec((bk, tn), lambda e, j: ((e + blocks - (nb - 1)) % blocks, j)),
            pl.BlockSpec((bk, tn), lambda e, j: ((e + blocks - nb) % blocks, j)),
        ],
        out_specs=pl.BlockSpec((1, n, tn), lambda e, j: (e, 0, j)),
        out_shape=jax.ShapeDtypeStruct((2 * nb - 1, n, cols), F32),
        compiler_params=_params("parallel", "parallel"),
    )(fl, fr, filt, filt)


def _block_conv_kernel(fb_ref, gb_ref, u_ref, gate_ref, skip_ref, h_ref, o_ref, u_scr, y_scr, *, rc):
    n, bk = fb_ref.shape
    nb = u_ref.shape[0] // bk
    for jb in range(nb):
        u_scr[jb] = _dot(fb_ref[...], u_ref[jb * bk:(jb + 1) * bk, :])
    row0 = lax.broadcasted_iota(jnp.int32, (rc, 1), 0) == 0
    for ib in range(nb):
        for r in range(0, bk, rc):
            re = slice(r, r + rc)
            im = slice(bk + r, bk + r + rc)
            acc_re = acc_im = None
            for jb in range(nb):
                dlt = ib - jb + nb - 1
                ure, uim = u_scr[jb, re, :], u_scr[jb, im, :]
                hre, him = h_ref[dlt, re, :], h_ref[dlt, im, :]
                if r == 0:
                    t_re = ure * hre - uim * jnp.where(row0, 0.0, him)
                    t_im = jnp.where(row0, 0.0, ure) * him + uim * jnp.where(row0, him, hre)
                else:
                    t_re = ure * hre - uim * him
                    t_im = ure * him + uim * hre
                acc_re = t_re if acc_re is None else acc_re + t_re
                acc_im = t_im if acc_im is None else acc_im + t_im
            y_scr[ib, re, :] = acc_re.astype(y_scr.dtype)
            y_scr[ib, im, :] = acc_im.astype(y_scr.dtype)
        y = _dot(gb_ref[...], y_scr[ib])
        rows = slice(ib * bk, (ib + 1) * bk)
        u = u_ref[rows, :].astype(F32)
        o_ref[rows, :] = (gate_ref[rows, :].astype(F32) * (y + u * skip_ref[...])).astype(o_ref.dtype)


def _block_conv(fb, gb, u, gate, gate_col, skip, spec, order, batch, seq, d, *, tn=256, rc=32):
    n, bk = fb.shape
    nb = seq // bk
    cb = d // tn
    return pl.pallas_call(
        functools.partial(_block_conv_kernel, rc=rc),
        grid=(cb, batch),
        in_specs=[
            pl.BlockSpec((n, bk), lambda j, b: (0, 0)),
            pl.BlockSpec((bk, n), lambda j, b: (0, 0)),
            pl.BlockSpec((seq, tn), lambda j, b: (b, j)),
            pl.BlockSpec((seq, tn), lambda j, b: (b, gate_col * cb + j)),
            pl.BlockSpec((1, tn), lambda j, b: (0, j)),
            pl.BlockSpec((2 * nb - 1, n, tn), lambda j, b: (0, 0, order * cb + j)),
        ],
        out_specs=pl.BlockSpec((seq, tn), lambda j, b: (b, j)),
        out_shape=jax.ShapeDtypeStruct((batch * seq, d), BF16),
        scratch_shapes=[pltpu.VMEM((nb, n, tn), F32), pltpu.VMEM((nb, n, tn), BF16)],
        compiler_params=_params("parallel", "arbitrary"),
    )(fb, gb, u, gate, skip, spec)


def kernel(x, norm_mix_g, norm_ffn_g, a_w_in, a_conv_w, a_w_out, b_w_qkv, b_q_norm_g, b_k_norm_g,
           b_rpb, b_w_out, c_w_in, c_short_w, c_short_b, c_f_w1, c_f_b1, c_f_w2, c_f_b2, c_f_w3,
           c_f_freq, c_f_skip, c_w_out, f_w13, f_w2):
    batch, seq, d = x.shape
    depth = norm_mix_g.shape[0]
    heads = d // NA_HEAD_DIM
    h = x.reshape(batch * seq, d)
    bf = lambda w: w.astype(BF16)
    ia = ib = ic = 0
    for i in range(depth):
        g_mix = norm_mix_g[i][None, :]
        kind = i % 3
        if kind == 0:
            h = _sconv_mixer(h, g_mix, bf(a_w_in[ia]), a_conv_w[ia], bf(a_w_out[ia]), seq)
            ia += 1
        elif kind == 1:
            head_gain = jnp.concatenate([
                jnp.tile(b_q_norm_g[ib], heads) * (NA_HEAD_DIM ** -0.5),
                jnp.tile(b_k_norm_g[ib], heads)])[None, :]
            qkv = _qkv(h, g_mix, bf(b_w_qkv[ib]), head_gain)
            h = _natten(qkv.reshape(batch, seq, 3 * d), _bias_table(b_rpb[ib]), bf(b_w_out[ib]),
                        h.reshape(batch, seq, d)).reshape(batch * seq, d)
            ib += 1
        else:
            fl, fr, gb = _block_dft_matrices(seq // HYENA_CONV_BLOCKS)
            filt = _hyena_filters_time(seq, d, c_f_w1[ic], c_f_b1[ic], c_f_w2[ic], c_f_b2[ic],
                                       c_f_w3[ic], c_f_freq[ic])
            spec = _filter_spectra(fl, fr, filt)
            vxx = _proj_conv(h, g_mix, bf(c_w_in[ic]), c_short_w[ic], c_short_b[ic][None, :], seq)
            skip = c_f_skip[ic]
            z = _block_conv(fl, gb, vxx, vxx, 1, skip[0:1], spec, 0, batch, seq, d)
            z = _block_conv(fl, gb, z, vxx, 2, skip[1:2], spec, 1, batch, seq, d)
            h = _matmul_residual(z, bf(c_w_out[ic]), h)
            ic += 1
        h = _ffn(h, norm_ffn_g[i][None, :], bf(f_w13[i]), bf(f_w2[i]))
    return h.reshape(batch, seq, d)
```

```python
import functools
import math

import jax
import jax.numpy as jnp
from jax import lax
from jax.experimental import pallas as pl
from jax.experimental.pallas import tpu as pltpu

F32 = jnp.float32
BF16 = jnp.bfloat16
HIGHEST = lax.Precision.HIGHEST

RMS_EPS = 1e-6
GRID_W = 64
NA_HEAD_DIM = 64
NA_WIN_ROWS = 8
NA_WIN_COLS = 16
NA_RPB_ROWS = 2 * NA_WIN_ROWS - 1
NA_RPB_COLS = 2 * NA_WIN_COLS - 1
NA_VALUES_LAG = 3
HYENA_BANDS = 16
HYENA_DECAY_TARGET = 1e-2
HYENA_FAST_DECAY = 0.3
HYENA_SLOW_DECAY = 1.5
HYENA_CONV_BLOCKS = 4
MASK_VALUE = -1e30

V7X_LANES = 128
V7X_SUBLANES = 8
V7X_MXU_DIM = 256
VMEM_LIMIT_BYTES = 56 * 1024 * 1024


def _params(*semantics):
    return pltpu.CompilerParams(dimension_semantics=semantics,
                                vmem_limit_bytes=VMEM_LIMIT_BYTES)


def _rmsnorm_bf16(x, g):
    ms = jnp.mean(x * x, axis=-1, keepdims=True)
    return (x * lax.rsqrt(ms + RMS_EPS) * g).astype(BF16)


def _dot(a, b):
    return jnp.dot(a, b, preferred_element_type=F32)


def _qkv_kernel(x_ref, g_ref, w_ref, hg_ref, o_ref):
    xn = _rmsnorm_bf16(x_ref[...], g_ref[...])
    nc = V7X_MXU_DIM
    qk_cols = hg_ref.shape[1]
    r = lax.broadcasted_iota(jnp.int32, (nc, nc), 0) // NA_HEAD_DIM
    c = lax.broadcasted_iota(jnp.int32, (nc, nc), 1) // NA_HEAD_DIM
    head_ones = jnp.where(r == c, 1.0, 0.0).astype(BF16)
    chunks = list(range(0, w_ref.shape[1], nc))

    def head_sumsq(n, acc):
        return _dot((acc * acc).astype(BF16), head_ones) if n < qk_cols else None

    def finish(n, acc, ssq):
        if ssq is not None:
            acc = acc * lax.rsqrt(ssq * (1.0 / NA_HEAD_DIM) + RMS_EPS) * hg_ref[:, n:n + nc]
        o_ref[:, n:n + nc] = acc.astype(o_ref.dtype)

    acc_q, ssq_q = {}, {}
    for t in range(len(chunks) + 2):
        if t < len(chunks):
            acc_q[t] = _dot(xn, w_ref[:, chunks[t]:chunks[t] + nc])
        if 1 <= t <= len(chunks):
            ssq_q[t - 1] = head_sumsq(chunks[t - 1], acc_q[t - 1])
        if t >= 2:
            finish(chunks[t - 2], acc_q.pop(t - 2), ssq_q.pop(t - 2))


def _qkv(x, g, w, head_gain, *, tm=512):
    t, d = x.shape
    n = w.shape[1]
    return pl.pallas_call(
        _qkv_kernel,
        grid=(t // tm,),
        in_specs=[
            pl.BlockSpec((tm, d), lambda i: (i, 0)),
            pl.BlockSpec((1, d), lambda i: (0, 0)),
            pl.BlockSpec((d, n), lambda i: (0, 0)),
            pl.BlockSpec(head_gain.shape, lambda i: (0, 0)),
        ],
        out_specs=pl.BlockSpec((tm, n), lambda i: (i, 0)),
        out_shape=jax.ShapeDtypeStruct((t, n), BF16),
        compiler_params=_params("parallel"),
    )(x, g, w, head_gain)


def _matmul_residual_kernel(a_ref, w_ref, x_ref, o_ref):
    o_ref[...] = x_ref[...] + _dot(a_ref[...], w_ref[...])


def _matmul_residual(a, w, x, *, tm=512):
    t, k = a.shape
    n = w.shape[1]
    return pl.pallas_call(
        _matmul_residual_kernel,
        grid=(t // tm,),
        in_specs=[
            pl.BlockSpec((tm, k), lambda i: (i, 0)),
            pl.BlockSpec((k, n), lambda i: (0, 0)),
            pl.BlockSpec((tm, n), lambda i: (i, 0)),
        ],
        out_specs=pl.BlockSpec((tm, n), lambda i: (i, 0)),
        out_shape=jax.ShapeDtypeStruct((t, n), F32),
        compiler_params=_params("parallel"),
    )(a, w, x)


def _ffn_kernel(x_ref, g_ref, w13_ref, w2_ref, o_ref, acc_ref, *, hc):
    x = x_ref[...]
    xn = _rmsnorm_bf16(x, g_ref[...])
    hidden = w2_ref.shape[0]
    for c in range(0, hidden, hc):
        gate = _dot(xn, w13_ref[:, c:c + hc])
        up = _dot(xn, w13_ref[:, hidden + c:hidden + c + hc])
        act = (gate * jax.nn.sigmoid(gate) * up).astype(BF16)
        part = _dot(act, w2_ref[c:c + hc, :])
        if c == 0:
            acc_ref[...] = part
        else:
            acc_ref[...] += part
    o_ref[...] = x + acc_ref[...]


def _ffn(x, g, w13, w2, layer, *, tm=512, hc=256):
    t, d = x.shape
    hidden = w2.shape[1]
    return pl.pallas_call(
        functools.partial(_ffn_kernel, hc=hc),
        grid=(t // tm,),
        in_specs=[
            pl.BlockSpec((tm, d), lambda i: (i, 0)),
            pl.BlockSpec((1, d), lambda i: (0, 0)),
            pl.BlockSpec((None, d, 2 * hidden), lambda i: (layer, 0, 0)),
            pl.BlockSpec((None, hidden, d), lambda i: (layer, 0, 0)),
        ],
        out_specs=pl.BlockSpec((tm, d), lambda i: (i, 0)),
        out_shape=jax.ShapeDtypeStruct((t, d), F32),
        scratch_shapes=[pltpu.VMEM((tm, d), F32)],
        compiler_params=_params("parallel"),
    )(x, g, w13, w2)


def _shift_rows(v, prev_row, next_row):
    tm, n = v.shape
    sub = V7X_SUBLANES
    groups = tm // sub
    v3 = v.reshape(groups, sub, n)
    row = lax.broadcasted_iota(jnp.int32, (1, sub, 1), 1)
    down = pltpu.roll(v3, 1, axis=1)
    up = pltpu.roll(v3, sub - 1, axis=1)
    halo = lambda r: jnp.broadcast_to(r[None], (1, sub, n))
    down_nb = jnp.concatenate([halo(prev_row), down[:groups - 1]], axis=0)
    up_nb = jnp.concatenate([up[1:], halo(next_row)], axis=0)
    v_prev = jnp.where(row == 0, down_nb, down)
    v_next = jnp.where(row == sub - 1, up_nb, up)
    return v_prev.reshape(tm, n), v_next.reshape(tm, n)


def _conv3(v, prev_row, next_row, w):
    v_prev, v_next = _shift_rows(v, prev_row, next_row)
    return w[0:1] * v_prev + w[1:2] * v + w[2:3] * v_next


def _finished_tile_edges(i, tiles_per_seq):
    pos = (i - 1) % tiles_per_seq
    return pos == 0, pos == tiles_per_seq - 1


def _delayed_specs(tm, n_tiles):
    current = lambda i: (jnp.minimum(i, n_tiles - 1), 0)
    finished = lambda i: (jnp.maximum(i - 1, 0), 0)
    const = lambda i: (0, 0)
    return current, finished, const


def _sconv_mixer_kernel(x_ref, xf_ref, g_ref, win_ref, cw_ref, wo_ref, o_ref, b_scr, v_scr, edge_scr,
                        *, tiles_per_seq):
    i = pl.program_id(0)
    tm, d = x_ref.shape

    @pl.when(i == 0)
    def _init():
        b_scr[...] = jnp.zeros_like(b_scr)
        v_scr[...] = jnp.zeros_like(v_scr)
        edge_scr[...] = jnp.zeros_like(edge_scr)

    xn = _rmsnorm_bf16(x_ref[...], g_ref[...])
    b_new = _dot(xn, win_ref[:, 0:d]).astype(BF16)
    v_new = _dot(xn, win_ref[:, d:2 * d]) * _dot(xn, win_ref[:, 2 * d:3 * d])

    first, last = _finished_tile_edges(i, tiles_per_seq)
    v = v_scr[...]
    prev_row = jnp.where(first, 0.0, edge_scr[...])
    next_row = jnp.where(last, 0.0, v_new[0:1, :])
    conv = _conv3(v, prev_row, next_row, cw_ref[...])
    y = (b_scr[...].astype(F32) * conv).astype(BF16)
    o_ref[...] = xf_ref[...] + _dot(y, wo_ref[...])

    edge_scr[...] = v[tm - 1:tm, :]
    b_scr[...] = b_new
    v_scr[...] = v_new


def _sconv_mixer(x, g, w_in, conv_w, w_out, layer, seq, *, tm=512):
    t, d = x.shape
    n_tiles = t // tm
    current, finished, const = _delayed_specs(tm, n_tiles)
    stacked = lambda i: (layer, 0, 0)
    return pl.pallas_call(
        functools.partial(_sconv_mixer_kernel, tiles_per_seq=seq // tm),
        grid=(n_tiles + 1,),
        in_specs=[
            pl.BlockSpec((tm, d), current),
            pl.BlockSpec((tm, d), finished),
            pl.BlockSpec((1, d), const),
            pl.BlockSpec((None, d, 3 * d), stacked),
            pl.BlockSpec((3, d), const),
            pl.BlockSpec((None, d, d), stacked),
        ],
        out_specs=pl.BlockSpec((tm, d), finished),
        out_shape=jax.ShapeDtypeStruct((t, d), F32),
        scratch_shapes=[pltpu.VMEM((tm, d), BF16),
                        pltpu.VMEM((tm, d), F32),
                        pltpu.VMEM((1, d), F32)],
        compiler_params=_params("arbitrary"),
    )(x, x, g, w_in, conv_w, w_out)


def _proj_conv_kernel(x_ref, g_ref, w_ref, cw_ref, cb_ref, o_ref, p_scr, edge_scr, *, tiles_per_seq, nc):
    i = pl.program_id(0)
    tm = x_ref.shape[0]

    @pl.when(i == 0)
    def _init():
        p_scr[...] = jnp.zeros_like(p_scr)
        edge_scr[...] = jnp.zeros_like(edge_scr)

    xn = _rmsnorm_bf16(x_ref[...], g_ref[...])
    first, last = _finished_tile_edges(i, tiles_per_seq)
    for c in range(0, w_ref.shape[1], nc):
        cols = slice(c, c + nc)
        p_new = _dot(xn, w_ref[:, cols])
        p = p_scr[:, cols]
        prev_row = jnp.where(first, 0.0, edge_scr[:, cols])
        next_row = jnp.where(last, 0.0, p_new[0:1, :])
        conv = _conv3(p, prev_row, next_row, cw_ref[:, cols])
        o_ref[:, cols] = (conv + cb_ref[:, cols]).astype(o_ref.dtype)
        edge_scr[:, cols] = p[tm - 1:tm, :]
        p_scr[:, cols] = p_new


def _proj_conv(x, g, w, conv_w, conv_b, seq, *, tm=512, nc=512):
    t, d = x.shape
    n = w.shape[1]
    n_tiles = t // tm
    current, finished, const = _delayed_specs(tm, n_tiles)
    return pl.pallas_call(
        functools.partial(_proj_conv_kernel, tiles_per_seq=seq // tm, nc=nc),
        grid=(n_tiles + 1,),
        in_specs=[
            pl.BlockSpec((tm, d), current),
            pl.BlockSpec((1, d), const),
            pl.BlockSpec((d, n), const),
            pl.BlockSpec((3, n), const),
            pl.BlockSpec((1, n), const),
        ],
        out_specs=pl.BlockSpec((tm, n), finished),
        out_shape=jax.ShapeDtypeStruct((t, n), BF16),
        scratch_shapes=[pltpu.VMEM((tm, n), F32),
                        pltpu.VMEM((1, n), F32)],
        compiler_params=_params("arbitrary"),
    )(x, g, w, conv_w, conv_b)


def _bias_table_kernel(r_ref, o_ref):
    kdim = r_ref.shape[1]
    n = o_ref.shape[1]

    def decode(shape):
        c = lax.broadcasted_iota(jnp.int32, shape, 1)
        qc = c // (2 * GRID_W)
        lane = c % (2 * GRID_W)
        second = lane >= GRID_W
        kc = jnp.where(second, lane - GRID_W, lane)
        start = jnp.clip(qc - NA_WIN_COLS // 2, 0, GRID_W - NA_WIN_COLS)
        valid = (kc >= start) & (kc < start + NA_WIN_COLS)
        idx = kc - qc + (NA_WIN_COLS - 1) + jnp.where(second, NA_RPB_COLS, 0)
        return valid, idx

    valid, idx = decode((kdim, n))
    i = lax.broadcasted_iota(jnp.int32, (kdim, n), 0)
    onehot = jnp.where(valid & (i == idx), 1.0, 0.0)
    tab = jnp.dot(r_ref[...], onehot, preferred_element_type=F32, precision=HIGHEST)
    valid_row, _ = decode((1, n))
    o_ref[...] = jnp.where(valid_row, tab, MASK_VALUE)


def _bias_table(rpb):
    heads = rpb.shape[0]
    pairs = NA_RPB_ROWS - 1
    kdim = 64
    pad = jnp.zeros((pairs, heads, kdim - 2 * NA_RPB_COLS), F32)
    rows = jnp.swapaxes(rpb, 0, 1)
    r = jnp.concatenate([rows[:-1], rows[1:], pad], axis=-1).reshape(pairs * heads, kdim)
    n = GRID_W * 2 * GRID_W
    tab = pl.pallas_call(
        _bias_table_kernel,
        out_shape=jax.ShapeDtypeStruct((pairs * heads, n), F32),
        compiler_params=pltpu.CompilerParams(vmem_limit_bytes=VMEM_LIMIT_BYTES),
    )(r)
    return tab.reshape(pairs, heads * GRID_W, 2 * GRID_W)


def _natten_kernel(q_ref, k_ref, v_ref, tab_ref, wo_ref, x_ref, o_ref, att_scr, *, rows_per_step, n_rows):
    pairs = q_ref.shape[2] // V7X_LANES
    lane = lax.broadcasted_iota(jnp.int32, (GRID_W, V7X_LANES), 1)
    low = lane < NA_HEAD_DIM
    win = NA_WIN_ROWS * GRID_W

    def row_body(rr, carry):
        r = pl.program_id(1) * rows_per_step + rr
        rs = jnp.clip(r - NA_WIN_ROWS // 2, 0, n_rows - NA_WIN_ROWS)
        d0 = rs - r + NA_WIN_ROWS - 1
        q_rows = pl.ds(pl.multiple_of(rr * GRID_W, GRID_W), GRID_W)
        k_rows = pl.ds(pl.multiple_of(rs * GRID_W, GRID_W), win)

        def scores(hp):
            cols = slice(hp * V7X_LANES, (hp + 1) * V7X_LANES)
            q2 = q_ref[0, q_rows, cols]
            zero = jnp.zeros_like(q2)
            qs = jnp.concatenate([jnp.where(low, q2, zero), jnp.where(low, zero, q2)], axis=0)
            return lax.dot_general(qs, k_ref[0, k_rows, cols], (((1,), (1,)), ((), ())),
                                   preferred_element_type=F32)

        def softmax(hp, s):
            rows = slice(hp * 2 * GRID_W, (hp + 1) * 2 * GRID_W)
            bias = jnp.concatenate(
                [tab_ref[pl.ds(d0 + 2 * p, 1), rows, :][0] for p in range(NA_WIN_ROWS // 2)], axis=1)
            s = s + bias
            m = jnp.max(s, axis=-1, keepdims=True)
            e = jnp.exp(s - m)
            return e.astype(BF16), 1.0 / jnp.sum(e, axis=-1, keepdims=True)

        def values(hp, p, linv):
            cols = slice(hp * V7X_LANES, (hp + 1) * V7X_LANES)
            o = _dot(p, v_ref[0, k_rows, cols]) * linv
            att_scr[q_rows, cols] = jnp.where(low, o[:GRID_W], o[GRID_W:]).astype(att_scr.dtype)

        s_q, p_q = {}, {}
        for t in range(pairs + 1 + NA_VALUES_LAG):
            if t < pairs:
                s_q[t] = scores(t)
            if 1 <= t <= pairs:
                p_q[t - 1] = softmax(t - 1, s_q.pop(t - 1))
            if t >= 1 + NA_VALUES_LAG:
                hp = t - 1 - NA_VALUES_LAG
                values(hp, *p_q.pop(hp))
        return carry

    lax.fori_loop(0, rows_per_step, row_body, 0)
    o_ref[0] = x_ref[0] + _dot(att_scr[...], wo_ref[...])


def _natten(qkv, tab, w_out, x, *, rows_per_step=8):
    b, s, d3 = qkv.shape
    d = d3 // 3
    n_rows = s // GRID_W
    qs = rows_per_step * GRID_W
    return pl.pallas_call(
        functools.partial(_natten_kernel, rows_per_step=rows_per_step, n_rows=n_rows),
        grid=(b, n_rows // rows_per_step),
        in_specs=[
            pl.BlockSpec((1, qs, d), lambda i, j: (i, j, 0)),
            pl.BlockSpec((1, s, d), lambda i, j: (i, 0, 1)),
            pl.BlockSpec((1, s, d), lambda i, j: (i, 0, 2)),
            pl.BlockSpec(tab.shape, lambda i, j: (0, 0, 0)),
            pl.BlockSpec((d, d), lambda i, j: (0, 0)),
            pl.BlockSpec((1, qs, d), lambda i, j: (i, j, 0)),
        ],
        out_specs=pl.BlockSpec((1, qs, d), lambda i, j: (i, j, 0)),
        out_shape=jax.ShapeDtypeStruct((b, s, d), F32),
        scratch_shapes=[pltpu.VMEM((qs, d), BF16)],
        compiler_params=_params("parallel", "arbitrary"),
    )(qkv, qkv, qkv, tab, w_out, x)


def _block_dft_matrices(bk):
    n = 2 * bk
    row = jnp.arange(n, dtype=jnp.int32)
    keff = jnp.where(row <= bk, row, row - bk)
    quarter = jnp.where(row > bk, n // 4, 0)
    t = jnp.arange(n, dtype=jnp.int32)
    turns = (keff[:, None] * t[None, :] + quarter[:, None]) % n
    full = jnp.cos(turns.astype(F32) * (2.0 * math.pi / n))
    weight = jnp.where((row == 0) | (row == bk), 1.0 / n, 2.0 / n)
    inv = (full[:, :bk] * weight[:, None]).T
    return full[:, :bk].astype(BF16), full[:, bk:].astype(BF16), inv.astype(BF16)


def _filter_kernel(bands_ref, w1t_ref, w1c_ref, w1s_ref, b1_ref, w2_ref, b2_ref, freq_ref,
                   w3f_ref, w3r_ref, delta_ref, h_ref, hid_scr):
    seq = h_ref.shape[0] // 2
    j = lax.broadcasted_iota(jnp.int32, (seq, 1), 0)
    pos_f = j.astype(F32)
    pos_r = jnp.where(j == 0, 0, seq - j).astype(F32)

    def mlp(pos):
        t = pos / (seq - 1.0)
        ang = (2.0 * math.pi) * pos / seq * bands_ref[...]
        pre = (t * w1t_ref[...]
               + jnp.dot(jnp.cos(ang), w1c_ref[...], preferred_element_type=F32, precision=HIGHEST)
               + jnp.dot(-jnp.sin(ang), w1s_ref[...], preferred_element_type=F32, precision=HIGHEST)
               + b1_ref[...])
        hid = jnp.sin(freq_ref[...] * pre)
        return jnp.sin(freq_ref[...] * (
            jnp.dot(hid, w2_ref[...], preferred_element_type=F32, precision=HIGHEST) + b2_ref[...]))

    @pl.when((pl.program_id(0) == 0) & (pl.program_id(1) == 0))
    def _hidden():
        hid_scr[0] = mlp(pos_f)
        hid_scr[1] = mlp(pos_r)

    def filt(hid, w3_ref, pos):
        f = jnp.dot(hid, w3_ref[...], preferred_element_type=F32, precision=HIGHEST)
        return f * jnp.exp(-(pos / (seq - 1.0)) * delta_ref[...])

    fwd = filt(hid_scr[0], w3f_ref, pos_f)
    rev = filt(hid_scr[1], w3r_ref, pos_r)
    top = fwd + jnp.where(j == 0, rev, 0.0)
    bot = jnp.where(j == 0, 0.0, rev)
    l1 = jnp.sum(jnp.abs(top), axis=0, keepdims=True) + jnp.sum(jnp.abs(bot), axis=0, keepdims=True)
    inv = 1.0 / l1
    h_ref[0:seq, :] = (top * inv).astype(h_ref.dtype)
    h_ref[seq:2 * seq, :] = (bot * inv).astype(h_ref.dtype)


def _hyena_filters_time(seq, d, w1, b1, w2, b2, w3, freq, *, tn=512):
    nb = HYENA_BANDS
    hid = w2.shape[0]
    bands = jnp.linspace(1e-4, nb - 1, nb, dtype=F32)[None, :]
    lt = math.log(HYENA_DECAY_TARGET)
    deltas = jnp.abs(jnp.linspace(lt / HYENA_SLOW_DECAY, lt / HYENA_FAST_DECAY, d, dtype=F32))[None, :]
    small = lambda shape: pl.BlockSpec(shape, lambda o, c: (0, 0))
    cpo = d // tn
    return pl.pallas_call(
        _filter_kernel,
        grid=(2, cpo),
        in_specs=[
            small((1, nb)), small((1, hid)), small((nb, hid)), small((nb, hid)), small((1, hid)),
            small((hid, hid)), small((1, hid)), small((1, hid)),
            pl.BlockSpec((hid, tn), lambda o, c: (0, 2 * o * cpo + c)),
            pl.BlockSpec((hid, tn), lambda o, c: (0, (2 * o + 1) * cpo + c)),
            pl.BlockSpec((1, tn), lambda o, c: (0, c)),
        ],
        out_specs=pl.BlockSpec((2 * seq, tn), lambda o, c: (0, o * cpo + c)),
        out_shape=jax.ShapeDtypeStruct((2 * seq, 2 * d), BF16),
        scratch_shapes=[pltpu.VMEM((2, seq, hid), F32)],
        compiler_params=_params("arbitrary", "arbitrary"),
    )(bands, w1[0:1], w1[1:1 + nb], w1[1 + nb:1 + 2 * nb], b1[None, :], w2, b2[None, :],
      freq[None, :], w3, w3, deltas)


def _filter_spectra_kernel(fl_ref, fr_ref, top_ref, bot_ref, o_ref):
    o_ref[0] = _dot(fl_ref[...], top_ref[...]) + _dot(fr_ref[...], bot_ref[...])


def _filter_spectra(fl, fr, filt, *, tn=1024):
    n, bk = fl.shape
    blocks = filt.shape[0] // bk
    nb = blocks // 2
    cols = filt.shape[1]
    half = pl.BlockSpec((n, bk), lambda e, j: (0, 0))
    return pl.pallas_call(
        _filter_spectra_kernel,
        grid=(2 * nb - 1, cols // tn),
        in_specs=[
            half, half,
            pl.BlockSpec((bk, tn), lambda e, j: ((e + blocks - (nb - 1)) % blocks, j)),
            pl.BlockSpec((bk, tn), lambda e, j: ((e + blocks - nb) % blocks, j)),
        ],
        out_specs=pl.BlockSpec((1, n, tn), lambda e, j: (e, 0, j)),
        out_shape=jax.ShapeDtypeStruct((2 * nb - 1, n, cols), F32),
        compiler_params=_params("parallel", "parallel"),
    )(fl, fr, filt, filt)


def _block_conv_kernel(fb_ref, gb_ref, u_ref, gate_ref, skip_ref, h_ref, o_ref, u_scr, y_scr, *, rc):
    n, bk = fb_ref.shape
    nb = u_ref.shape[0] // bk
    for jb in range(nb):
        u_scr[jb] = _dot(fb_ref[...], u_ref[jb * bk:(jb + 1) * bk, :])
    row0 = lax.broadcasted_iota(jnp.int32, (rc, 1), 0) == 0
    for ib in range(nb):
        for r in range(0, bk, rc):
            re = slice(r, r + rc)
            im = slice(bk + r, bk + r + rc)
            acc_re = acc_im = None
            for jb in range(nb):
                dlt = ib - jb + nb - 1
                ure, uim = u_scr[jb, re, :], u_scr[jb, im, :]
                hre, him = h_ref[dlt, re, :], h_ref[dlt, im, :]
                if r == 0:
                    t_re = ure * hre - uim * jnp.where(row0, 0.0, him)
                    t_im = jnp.where(row0, 0.0, ure) * him + uim * jnp.where(row0, him, hre)
                else:
                    t_re = ure * hre - uim * him
                    t_im = ure * him + uim * hre
                acc_re = t_re if acc_re is None else acc_re + t_re
                acc_im = t_im if acc_im is None else acc_im + t_im
            y_scr[ib, re, :] = acc_re.astype(y_scr.dtype)
            y_scr[ib, im, :] = acc_im.astype(y_scr.dtype)
        y = _dot(gb_ref[...], y_scr[ib])
        rows = slice(ib * bk, (ib + 1) * bk)
        u = u_ref[rows, :].astype(F32)
        o_ref[rows, :] = (gate_ref[rows, :].astype(F32) * (y + u * skip_ref[...])).astype(o_ref.dtype)


def _block_conv(fb, gb, u, gate, gate_col, skip, spec, order, batch, seq, d, *, tn=256, rc=32):
    n, bk = fb.shape
    nb = seq // bk
    cb = d // tn
    return pl.pallas_call(
        functools.partial(_block_conv_kernel, rc=rc),
        grid=(cb, batch),
        in_specs=[
            pl.BlockSpec((n, bk), lambda j, b: (0, 0)),
            pl.BlockSpec((bk, n), lambda j, b: (0, 0)),
            pl.BlockSpec((seq, tn), lambda j, b: (b, j)),
            pl.BlockSpec((seq, tn), lambda j, b: (b, gate_col * cb + j)),
            pl.BlockSpec((1, tn), lambda j, b: (0, j)),
            pl.BlockSpec((2 * nb - 1, n, tn), lambda j, b: (0, 0, order * cb + j)),
        ],
        out_specs=pl.BlockSpec((seq, tn), lambda j, b: (b, j)),
        out_shape=jax.ShapeDtypeStruct((batch * seq, d), BF16),
        scratch_shapes=[pltpu.VMEM((nb, n, tn), F32), pltpu.VMEM((nb, n, tn), BF16)],
        compiler_params=_params("parallel", "arbitrary"),
    )(fb, gb, u, gate, skip, spec)


def kernel(x, norm_mix_g, norm_ffn_g, a_w_in, a_conv_w, a_w_out, b_w_qkv, b_q_norm_g, b_k_norm_g,
           b_rpb, b_w_out, c_w_in, c_short_w, c_short_b, c_f_w1, c_f_b1, c_f_w2, c_f_b2, c_f_w3,
           c_f_freq, c_f_skip, c_w_out, f_w13, f_w2):
    batch, seq, d = x.shape
    depth = norm_mix_g.shape[0]
    heads = d // NA_HEAD_DIM
    h = x.reshape(batch * seq, d)
    bf = lambda w: w.astype(BF16)
    a_w_in, a_w_out, f_w13, f_w2 = bf(a_w_in), bf(a_w_out), bf(f_w13), bf(f_w2)
    ia = ib = ic = 0
    for i in range(depth):
        g_mix = norm_mix_g[i][None, :]
        kind = i % 3
        if kind == 0:
            h = _sconv_mixer(h, g_mix, a_w_in, a_conv_w[ia], a_w_out, ia, seq)
            ia += 1
        elif kind == 1:
            head_gain = jnp.concatenate([
                jnp.tile(b_q_norm_g[ib], heads) * (NA_HEAD_DIM ** -0.5),
                jnp.tile(b_k_norm_g[ib], heads)])[None, :]
            qkv = _qkv(h, g_mix, bf(b_w_qkv[ib]), head_gain)
            h = _natten(qkv.reshape(batch, seq, 3 * d), _bias_table(b_rpb[ib]), bf(b_w_out[ib]),
                        h.reshape(batch, seq, d)).reshape(batch * seq, d)
            ib += 1
        else:
            fl, fr, gb = _block_dft_matrices(seq // HYENA_CONV_BLOCKS)
            filt = _hyena_filters_time(seq, d, c_f_w1[ic], c_f_b1[ic], c_f_w2[ic], c_f_b2[ic],
                                       c_f_w3[ic], c_f_freq[ic])
            spec = _filter_spectra(fl, fr, filt)
            vxx = _proj_conv(h, g_mix, bf(c_w_in[ic]), c_short_w[ic], c_short_b[ic][None, :], seq)
            skip = c_f_skip[ic]
            z = _block_conv(fl, gb, vxx, vxx, 1, skip[0:1], spec, 0, batch, seq, d)
            z = _block_conv(fl, gb, z, vxx, 2, skip[1:2], spec, 1, batch, seq, d)
            h = _matmul_residual(z, bf(c_w_out[ic]), h)
            ic += 1
        h = _ffn(h, norm_ffn_g[i][None, :], f_w13, f_w2, i)
    return h.reshape(batch, seq, d)
```

```python
import functools
import math

import jax
import jax.numpy as jnp
from jax import lax
from jax.experimental import pallas as pl
from jax.experimental.pallas import tpu as pltpu

F32 = jnp.float32
BF16 = jnp.bfloat16
HIGHEST = lax.Precision.HIGHEST

RMS_EPS = 1e-6
GRID_W = 64
NA_HEAD_DIM = 64
NA_WIN_ROWS = 8
NA_WIN_COLS = 16
NA_RPB_ROWS = 2 * NA_WIN_ROWS - 1
NA_RPB_COLS = 2 * NA_WIN_COLS - 1
NA_VALUES_LAG = 3
HYENA_BANDS = 16
HYENA_DECAY_TARGET = 1e-2
HYENA_FAST_DECAY = 0.3
HYENA_SLOW_DECAY = 1.5
HYENA_CONV_BLOCKS = 4
MASK_VALUE = -1e30

V7X_LANES = 128
V7X_SUBLANES = 8
V7X_MXU_DIM = 256
VMEM_LIMIT_BYTES = 56 * 1024 * 1024


def _params(*semantics):
    return pltpu.CompilerParams(dimension_semantics=semantics,
                                vmem_limit_bytes=VMEM_LIMIT_BYTES)


def _rmsnorm_bf16(x, g):
    ms = jnp.mean(x * x, axis=-1, keepdims=True)
    return (x * lax.rsqrt(ms + RMS_EPS) * g).astype(BF16)


def _dot(a, b):
    return jnp.dot(a, b, preferred_element_type=F32)


def _qkv_kernel(x_ref, g_ref, w_ref, hg_ref, o_ref):
    xn = _rmsnorm_bf16(x_ref[...], g_ref[...])
    nc = V7X_MXU_DIM
    qk_cols = hg_ref.shape[1]
    r = lax.broadcasted_iota(jnp.int32, (nc, nc), 0) // NA_HEAD_DIM
    c = lax.broadcasted_iota(jnp.int32, (nc, nc), 1) // NA_HEAD_DIM
    head_ones = jnp.where(r == c, 1.0, 0.0).astype(BF16)
    chunks = list(range(0, w_ref.shape[1], nc))

    def head_sumsq(n, acc):
        return _dot((acc * acc).astype(BF16), head_ones) if n < qk_cols else None

    def finish(n, acc, ssq):
        if ssq is not None:
            acc = acc * lax.rsqrt(ssq * (1.0 / NA_HEAD_DIM) + RMS_EPS) * hg_ref[:, n:n + nc]
        o_ref[:, n:n + nc] = acc.astype(o_ref.dtype)

    acc_q, ssq_q = {}, {}
    for t in range(len(chunks) + 2):
        if t < len(chunks):
            acc_q[t] = _dot(xn, w_ref[:, chunks[t]:chunks[t] + nc])
        if 1 <= t <= len(chunks):
            ssq_q[t - 1] = head_sumsq(chunks[t - 1], acc_q[t - 1])
        if t >= 2:
            finish(chunks[t - 2], acc_q.pop(t - 2), ssq_q.pop(t - 2))


def _qkv(x, g, w, head_gain, *, tm=512):
    t, d = x.shape
    n = w.shape[1]
    return pl.pallas_call(
        _qkv_kernel,
        grid=(t // tm,),
        in_specs=[
            pl.BlockSpec((tm, d), lambda i: (i, 0)),
            pl.BlockSpec((1, d), lambda i: (0, 0)),
            pl.BlockSpec((d, n), lambda i: (0, 0)),
            pl.BlockSpec(head_gain.shape, lambda i: (0, 0)),
        ],
        out_specs=pl.BlockSpec((tm, n), lambda i: (i, 0)),
        out_shape=jax.ShapeDtypeStruct((t, n), BF16),
        compiler_params=_params("parallel"),
    )(x, g, w, head_gain)


def _ffn_kernel(x_ref, g_ref, w13_ref, w2_ref, *rest, hc):
    o_ref, acc_ref = rest[-2:]
    x = x_ref[...]
    if len(rest) == 4:
        z_ref, wo_ref = rest[:2]
        x = x + _dot(z_ref[...], wo_ref[...])
    xn = _rmsnorm_bf16(x, g_ref[...])
    hidden = w2_ref.shape[0]
    for c in range(0, hidden, hc):
        gate = _dot(xn, w13_ref[:, c:c + hc])
        up = _dot(xn, w13_ref[:, hidden + c:hidden + c + hc])
        act = (gate * jax.nn.sigmoid(gate) * up).astype(BF16)
        part = _dot(act, w2_ref[c:c + hc, :])
        if c == 0:
            acc_ref[...] = part
        else:
            acc_ref[...] += part
    o_ref[...] = x + acc_ref[...]


def _ffn(x, g, w13, w2, layer, z=None, w_out=None, *, tm=512, hc=256):
    t, d = x.shape
    hidden = w2.shape[1]
    once = pl.Buffered(1)
    in_specs = [
        pl.BlockSpec((tm, d), lambda i: (i, 0)),
        pl.BlockSpec((1, d), lambda i: (0, 0)),
        pl.BlockSpec((None, d, 2 * hidden), lambda i: (layer, 0, 0), pipeline_mode=once),
        pl.BlockSpec((None, hidden, d), lambda i: (layer, 0, 0), pipeline_mode=once),
    ]
    args = [x, g, w13, w2]
    if z is not None:
        in_specs += [pl.BlockSpec((tm, d), lambda i: (i, 0)),
                     pl.BlockSpec((d, d), lambda i: (0, 0), pipeline_mode=once)]
        args += [z, w_out]
    return pl.pallas_call(
        functools.partial(_ffn_kernel, hc=hc),
        grid=(t // tm,),
        in_specs=in_specs,
        out_specs=pl.BlockSpec((tm, d), lambda i: (i, 0)),
        out_shape=jax.ShapeDtypeStruct((t, d), F32),
        scratch_shapes=[pltpu.VMEM((tm, d), F32)],
        compiler_params=_params("parallel"),
    )(*args)


def _shift_rows(v, prev_row, next_row):
    tm, n = v.shape
    sub = V7X_SUBLANES
    groups = tm // sub
    v3 = v.reshape(groups, sub, n)
    row = lax.broadcasted_iota(jnp.int32, (1, sub, 1), 1)
    down = pltpu.roll(v3, 1, axis=1)
    up = pltpu.roll(v3, sub - 1, axis=1)
    halo = lambda r: jnp.broadcast_to(r[None], (1, sub, n))
    down_nb = jnp.concatenate([halo(prev_row), down[:groups - 1]], axis=0)
    up_nb = jnp.concatenate([up[1:], halo(next_row)], axis=0)
    v_prev = jnp.where(row == 0, down_nb, down)
    v_next = jnp.where(row == sub - 1, up_nb, up)
    return v_prev.reshape(tm, n), v_next.reshape(tm, n)


def _conv3(v, prev_row, next_row, w):
    v_prev, v_next = _shift_rows(v, prev_row, next_row)
    return w[0:1] * v_prev + w[1:2] * v + w[2:3] * v_next


def _finished_tile_edges(i, tiles_per_seq):
    pos = (i - 1) % tiles_per_seq
    return pos == 0, pos == tiles_per_seq - 1


def _delayed_specs(tm, n_tiles):
    current = lambda i: (jnp.minimum(i, n_tiles - 1), 0)
    finished = lambda i: (jnp.maximum(i - 1, 0), 0)
    const = lambda i: (0, 0)
    return current, finished, const


def _sconv_mixer_kernel(x_ref, xf_ref, g_ref, win_ref, cw_ref, wo_ref, o_ref, b_scr, v_scr, edge_scr,
                        *, tiles_per_seq):
    i = pl.program_id(0)
    tm, d = x_ref.shape

    @pl.when(i == 0)
    def _init():
        b_scr[...] = jnp.zeros_like(b_scr)
        v_scr[...] = jnp.zeros_like(v_scr)
        edge_scr[...] = jnp.zeros_like(edge_scr)

    xn = _rmsnorm_bf16(x_ref[...], g_ref[...])
    b_new = _dot(xn, win_ref[:, 0:d]).astype(BF16)
    v_new = _dot(xn, win_ref[:, d:2 * d]) * _dot(xn, win_ref[:, 2 * d:3 * d])

    first, last = _finished_tile_edges(i, tiles_per_seq)
    v = v_scr[...]
    prev_row = jnp.where(first, 0.0, edge_scr[...])
    next_row = jnp.where(last, 0.0, v_new[0:1, :])
    conv = _conv3(v, prev_row, next_row, cw_ref[...])
    y = (b_scr[...].astype(F32) * conv).astype(BF16)
    o_ref[...] = xf_ref[...] + _dot(y, wo_ref[...])

    edge_scr[...] = v[tm - 1:tm, :]
    b_scr[...] = b_new
    v_scr[...] = v_new


def _sconv_mixer(x, g, w_in, conv_w, w_out, layer, seq, *, tm=512):
    t, d = x.shape
    n_tiles = t // tm
    current, finished, const = _delayed_specs(tm, n_tiles)
    stacked = lambda i: (layer, 0, 0)
    return pl.pallas_call(
        functools.partial(_sconv_mixer_kernel, tiles_per_seq=seq // tm),
        grid=(n_tiles + 1,),
        in_specs=[
            pl.BlockSpec((tm, d), current),
            pl.BlockSpec((tm, d), finished),
            pl.BlockSpec((1, d), const),
            pl.BlockSpec((None, d, 3 * d), stacked),
            pl.BlockSpec((3, d), const),
            pl.BlockSpec((None, d, d), stacked),
        ],
        out_specs=pl.BlockSpec((tm, d), finished),
        out_shape=jax.ShapeDtypeStruct((t, d), F32),
        scratch_shapes=[pltpu.VMEM((tm, d), BF16),
                        pltpu.VMEM((tm, d), F32),
                        pltpu.VMEM((1, d), F32)],
        compiler_params=_params("arbitrary"),
    )(x, x, g, w_in, conv_w, w_out)


def _proj_conv_kernel(x_ref, g_ref, w_ref, cw_ref, cb_ref, o_ref, p_scr, edge_scr, *, tiles_per_seq, nc):
    i = pl.program_id(0)
    tm = x_ref.shape[0]

    @pl.when(i == 0)
    def _init():
        p_scr[...] = jnp.zeros_like(p_scr)
        edge_scr[...] = jnp.zeros_like(edge_scr)

    xn = _rmsnorm_bf16(x_ref[...], g_ref[...])
    first, last = _finished_tile_edges(i, tiles_per_seq)
    for c in range(0, w_ref.shape[1], nc):
        cols = slice(c, c + nc)
        p_new = _dot(xn, w_ref[:, cols])
        p = p_scr[:, cols]
        prev_row = jnp.where(first, 0.0, edge_scr[:, cols])
        next_row = jnp.where(last, 0.0, p_new[0:1, :])
        conv = _conv3(p, prev_row, next_row, cw_ref[:, cols])
        o_ref[:, cols] = (conv + cb_ref[:, cols]).astype(o_ref.dtype)
        edge_scr[:, cols] = p[tm - 1:tm, :]
        p_scr[:, cols] = p_new


def _proj_conv(x, g, w, conv_w, conv_b, seq, *, tm=512, nc=512):
    t, d = x.shape
    n = w.shape[1]
    n_tiles = t // tm
    current, finished, const = _delayed_specs(tm, n_tiles)
    return pl.pallas_call(
        functools.partial(_proj_conv_kernel, tiles_per_seq=seq // tm, nc=nc),
        grid=(n_tiles + 1,),
        in_specs=[
            pl.BlockSpec((tm, d), current),
            pl.BlockSpec((1, d), const),
            pl.BlockSpec((d, n), const),
            pl.BlockSpec((3, n), const),
            pl.BlockSpec((1, n), const),
        ],
        out_specs=pl.BlockSpec((tm, n), finished),
        out_shape=jax.ShapeDtypeStruct((t, n), BF16),
        scratch_shapes=[pltpu.VMEM((tm, n), F32),
                        pltpu.VMEM((1, n), F32)],
        compiler_params=_params("arbitrary"),
    )(x, g, w, conv_w, conv_b)


def _bias_table_kernel(r_ref, o_ref):
    kdim = r_ref.shape[1]
    n = o_ref.shape[1]

    def decode(shape):
        c = lax.broadcasted_iota(jnp.int32, shape, 1)
        qc = c // (2 * GRID_W)
        lane = c % (2 * GRID_W)
        second = lane >= GRID_W
        kc = jnp.where(second, lane - GRID_W, lane)
        start = jnp.clip(qc - NA_WIN_COLS // 2, 0, GRID_W - NA_WIN_COLS)
        valid = (kc >= start) & (kc < start + NA_WIN_COLS)
        idx = kc - qc + (NA_WIN_COLS - 1) + jnp.where(second, NA_RPB_COLS, 0)
        return valid, idx

    valid, idx = decode((kdim, n))
    i = lax.broadcasted_iota(jnp.int32, (kdim, n), 0)
    onehot = jnp.where(valid & (i == idx), 1.0, 0.0)
    tab = jnp.dot(r_ref[...], onehot, preferred_element_type=F32, precision=HIGHEST)
    valid_row, _ = decode((1, n))
    o_ref[...] = jnp.where(valid_row, tab, MASK_VALUE)


def _bias_table(rpb):
    heads = rpb.shape[0]
    pairs = NA_RPB_ROWS - 1
    kdim = 64
    pad = jnp.zeros((pairs, heads, kdim - 2 * NA_RPB_COLS), F32)
    rows = jnp.swapaxes(rpb, 0, 1)
    r = jnp.concatenate([rows[:-1], rows[1:], pad], axis=-1).reshape(pairs * heads, kdim)
    n = GRID_W * 2 * GRID_W
    tab = pl.pallas_call(
        _bias_table_kernel,
        out_shape=jax.ShapeDtypeStruct((pairs * heads, n), F32),
        compiler_params=pltpu.CompilerParams(vmem_limit_bytes=VMEM_LIMIT_BYTES),
    )(r)
    return tab.reshape(pairs, heads * GRID_W, 2 * GRID_W)


def _natten_kernel(q_ref, k_ref, v_ref, tab_ref, wo_ref, x_ref, o_ref, att_scr, *, rows_per_step, n_rows):
    pairs = q_ref.shape[2] // V7X_LANES
    lane = lax.broadcasted_iota(jnp.int32, (GRID_W, V7X_LANES), 1)
    low = lane < NA_HEAD_DIM
    win = NA_WIN_ROWS * GRID_W

    def row_body(rr, carry):
        r = pl.program_id(1) * rows_per_step + rr
        rs = jnp.clip(r - NA_WIN_ROWS // 2, 0, n_rows - NA_WIN_ROWS)
        d0 = rs - r + NA_WIN_ROWS - 1
        q_rows = pl.ds(pl.multiple_of(rr * GRID_W, GRID_W), GRID_W)
        k_rows = pl.ds(pl.multiple_of(rs * GRID_W, GRID_W), win)

        def scores(hp):
            cols = slice(hp * V7X_LANES, (hp + 1) * V7X_LANES)
            q2 = q_ref[0, q_rows, cols]
            zero = jnp.zeros_like(q2)
            qs = jnp.concatenate([jnp.where(low, q2, zero), jnp.where(low, zero, q2)], axis=0)
            return lax.dot_general(qs, k_ref[0, k_rows, cols], (((1,), (1,)), ((), ())),
                                   preferred_element_type=F32)

        def softmax(hp, s):
            rows = slice(hp * 2 * GRID_W, (hp + 1) * 2 * GRID_W)
            bias = jnp.concatenate(
                [tab_ref[pl.ds(d0 + 2 * p, 1), rows, :][0] for p in range(NA_WIN_ROWS // 2)], axis=1)
            s = s + bias
            m = jnp.max(s, axis=-1, keepdims=True)
            e = jnp.exp(s - m)
            return e.astype(BF16), 1.0 / jnp.sum(e, axis=-1, keepdims=True)

        def values(hp, p, linv):
            cols = slice(hp * V7X_LANES, (hp + 1) * V7X_LANES)
            o = _dot(p, v_ref[0, k_rows, cols]) * linv
            att_scr[q_rows, cols] = jnp.where(low, o[:GRID_W], o[GRID_W:]).astype(att_scr.dtype)

        s_q, p_q = {}, {}
        for t in range(pairs + 1 + NA_VALUES_LAG):
            if t < pairs:
                s_q[t] = scores(t)
            if 1 <= t <= pairs:
                p_q[t - 1] = softmax(t - 1, s_q.pop(t - 1))
            if t >= 1 + NA_VALUES_LAG:
                hp = t - 1 - NA_VALUES_LAG
                values(hp, *p_q.pop(hp))
        return carry

    lax.fori_loop(0, rows_per_step, row_body, 0)
    o_ref[0] = x_ref[0] + _dot(att_scr[...], wo_ref[...])


def _natten(qkv, tab, w_out, x, *, rows_per_step=8):
    b, s, d3 = qkv.shape
    d = d3 // 3
    n_rows = s // GRID_W
    qs = rows_per_step * GRID_W
    return pl.pallas_call(
        functools.partial(_natten_kernel, rows_per_step=rows_per_step, n_rows=n_rows),
        grid=(b, n_rows // rows_per_step),
        in_specs=[
            pl.BlockSpec((1, qs, d), lambda i, j: (i, j, 0)),
            pl.BlockSpec((1, s, d), lambda i, j: (i, 0, 1)),
            pl.BlockSpec((1, s, d), lambda i, j: (i, 0, 2)),
            pl.BlockSpec(tab.shape, lambda i, j: (0, 0, 0)),
            pl.BlockSpec((d, d), lambda i, j: (0, 0)),
            pl.BlockSpec((1, qs, d), lambda i, j: (i, j, 0)),
        ],
        out_specs=pl.BlockSpec((1, qs, d), lambda i, j: (i, j, 0)),
        out_shape=jax.ShapeDtypeStruct((b, s, d), F32),
        scratch_shapes=[pltpu.VMEM((qs, d), BF16)],
        compiler_params=_params("parallel", "arbitrary"),
    )(qkv, qkv, qkv, tab, w_out, x)


def _block_dft_matrices(bk):
    n = 2 * bk
    row = jnp.arange(n, dtype=jnp.int32)
    keff = jnp.where(row <= bk, row, row - bk)
    quarter = jnp.where(row > bk, n // 4, 0)
    t = jnp.arange(n, dtype=jnp.int32)
    turns = (keff[:, None] * t[None, :] + quarter[:, None]) % n
    full = jnp.cos(turns.astype(F32) * (2.0 * math.pi / n))
    weight = jnp.where((row == 0) | (row == bk), 1.0 / n, 2.0 / n)
    inv = (full[:, :bk] * weight[:, None]).T
    return full[:, :bk].astype(BF16), full[:, bk:].astype(BF16), inv.astype(BF16)


def _filter_kernel(bands_ref, w1t_ref, w1c_ref, w1s_ref, b1_ref, w2_ref, b2_ref, freq_ref,
                   w3f_ref, w3r_ref, delta_ref, h_ref, hid_scr):
    seq = h_ref.shape[0] // 2
    j = lax.broadcasted_iota(jnp.int32, (seq, 1), 0)
    pos_f = j.astype(F32)
    pos_r = jnp.where(j == 0, 0, seq - j).astype(F32)

    def mlp(pos):
        t = pos / (seq - 1.0)
        ang = (2.0 * math.pi) * pos / seq * bands_ref[...]
        pre = (t * w1t_ref[...]
               + jnp.dot(jnp.cos(ang), w1c_ref[...], preferred_element_type=F32, precision=HIGHEST)
               + jnp.dot(-jnp.sin(ang), w1s_ref[...], preferred_element_type=F32, precision=HIGHEST)
               + b1_ref[...])
        hid = jnp.sin(freq_ref[...] * pre)
        return jnp.sin(freq_ref[...] * (
            jnp.dot(hid, w2_ref[...], preferred_element_type=F32, precision=HIGHEST) + b2_ref[...]))

    @pl.when((pl.program_id(0) == 0) & (pl.program_id(1) == 0))
    def _hidden():
        hid_scr[0] = mlp(pos_f)
        hid_scr[1] = mlp(pos_r)

    def filt(hid, w3_ref, pos):
        f = jnp.dot(hid, w3_ref[...], preferred_element_type=F32, precision=HIGHEST)
        return f * jnp.exp(-(pos / (seq - 1.0)) * delta_ref[...])

    fwd = filt(hid_scr[0], w3f_ref, pos_f)
    rev = filt(hid_scr[1], w3r_ref, pos_r)
    top = fwd + jnp.where(j == 0, rev, 0.0)
    bot = jnp.where(j == 0, 0.0, rev)
    l1 = jnp.sum(jnp.abs(top), axis=0, keepdims=True) + jnp.sum(jnp.abs(bot), axis=0, keepdims=True)
    inv = 1.0 / l1
    h_ref[0:seq, :] = (top * inv).astype(h_ref.dtype)
    h_ref[seq:2 * seq, :] = (bot * inv).astype(h_ref.dtype)


def _hyena_filters_time(seq, d, w1, b1, w2, b2, w3, freq, *, tn=512):
    nb = HYENA_BANDS
    hid = w2.shape[0]
    bands = jnp.linspace(1e-4, nb - 1, nb, dtype=F32)[None, :]
    lt = math.log(HYENA_DECAY_TARGET)
    deltas = jnp.abs(jnp.linspace(lt / HYENA_SLOW_DECAY, lt / HYENA_FAST_DECAY, d, dtype=F32))[None, :]
    small = lambda shape: pl.BlockSpec(shape, lambda o, c: (0, 0))
    cpo = d // tn
    return pl.pallas_call(
        _filter_kernel,
        grid=(2, cpo),
        in_specs=[
            small((1, nb)), small((1, hid)), small((nb, hid)), small((nb, hid)), small((1, hid)),
            small((hid, hid)), small((1, hid)), small((1, hid)),
            pl.BlockSpec((hid, tn), lambda o, c: (0, 2 * o * cpo + c)),
            pl.BlockSpec((hid, tn), lambda o, c: (0, (2 * o + 1) * cpo + c)),
            pl.BlockSpec((1, tn), lambda o, c: (0, c)),
        ],
        out_specs=pl.BlockSpec((2 * seq, tn), lambda o, c: (0, o * cpo + c)),
        out_shape=jax.ShapeDtypeStruct((2 * seq, 2 * d), BF16),
        scratch_shapes=[pltpu.VMEM((2, seq, hid), F32)],
        compiler_params=_params("arbitrary", "arbitrary"),
    )(bands, w1[0:1], w1[1:1 + nb], w1[1 + nb:1 + 2 * nb], b1[None, :], w2, b2[None, :],
      freq[None, :], w3, w3, deltas)


def _filter_spectra_kernel(fl_ref, fr_ref, top_ref, bot_ref, o_ref):
    o_ref[0] = _dot(fl_ref[...], top_ref[...]) + _dot(fr_ref[...], bot_ref[...])


def _filter_spectra(fl, fr, filt, *, tn=1024):
    n, bk = fl.shape
    blocks = filt.shape[0] // bk
    nb = blocks // 2
    cols = filt.shape[1]
    half = pl.BlockSpec((n, bk), lambda e, j: (0, 0))
    return pl.pallas_call(
        _filter_spectra_kernel,
        grid=(2 * nb - 1, cols // tn),
        in_specs=[
            half, half,
            pl.BlockSpec((bk, tn), lambda e, j: ((e + blocks - (nb - 1)) % blocks, j)),
            pl.BlockSpec((bk, tn), lambda e, j: ((e + blocks - nb) % blocks, j)),
        ],
        out_specs=pl.BlockSpec((1, n, tn), lambda e, j: (e, 0, j)),
        out_shape=jax.ShapeDtypeStruct((2 * nb - 1, n, cols), F32),
        compiler_params=_params("parallel", "parallel"),
    )(fl, fr, filt, filt)


def _block_conv_pass(fb_ref, gb_ref, u_ref, gate_ref, skip, h_ref, o_ref, u_scr, y_scr, rc):
    n, bk = fb_ref.shape
    nb = u_ref.shape[0] // bk
    for jb in range(nb):
        u_scr[jb] = _dot(fb_ref[...], u_ref[jb * bk:(jb + 1) * bk, :])
    row0 = lax.broadcasted_iota(jnp.int32, (rc, 1), 0) == 0
    for ib in range(nb):
        for r in range(0, bk, rc):
            re = slice(r, r + rc)
            im = slice(bk + r, bk + r + rc)
            acc_re = acc_im = None
            for jb in range(nb):
                dlt = ib - jb + nb - 1
                ure, uim = u_scr[jb, re, :], u_scr[jb, im, :]
                hre, him = h_ref[dlt, re, :], h_ref[dlt, im, :]
                if r == 0:
                    t_re = ure * hre - uim * jnp.where(row0, 0.0, him)
                    t_im = jnp.where(row0, 0.0, ure) * him + uim * jnp.where(row0, him, hre)
                else:
                    t_re = ure * hre - uim * him
                    t_im = ure * him + uim * hre
                acc_re = t_re if acc_re is None else acc_re + t_re
                acc_im = t_im if acc_im is None else acc_im + t_im
            y_scr[ib, re, :] = acc_re.astype(y_scr.dtype)
            y_scr[ib, im, :] = acc_im.astype(y_scr.dtype)
        y = _dot(gb_ref[...], y_scr[ib])
        rows = slice(ib * bk, (ib + 1) * bk)
        u = u_ref[rows, :].astype(F32)
        o_ref[rows, :] = (gate_ref[rows, :].astype(F32) * (y + u * skip)).astype(o_ref.dtype)


def _hyena_conv_kernel(fb_ref, gb_ref, v_ref, x1_ref, x2_ref, skip_ref, h1_ref, h2_ref, o_ref,
                       u_scr, y_scr, z_scr, *, rc):
    _block_conv_pass(fb_ref, gb_ref, v_ref, x1_ref, skip_ref[0:1, :], h1_ref, z_scr, u_scr, y_scr, rc)
    _block_conv_pass(fb_ref, gb_ref, z_scr, x2_ref, skip_ref[1:2, :], h2_ref, o_ref, u_scr, y_scr, rc)


def _hyena_conv(fb, gb, vxx, skip, spec, batch, seq, d, *, tn=256, rc=32):
    n, bk = fb.shape
    nb = seq // bk
    cb = d // tn
    part = lambda c: pl.BlockSpec((seq, tn), lambda j, b: (b, c * cb + j))
    spectra = lambda order: pl.BlockSpec((2 * nb - 1, n, tn), lambda j, b: (0, 0, order * cb + j))
    return pl.pallas_call(
        functools.partial(_hyena_conv_kernel, rc=rc),
        grid=(cb, batch),
        in_specs=[
            pl.BlockSpec((n, bk), lambda j, b: (0, 0)),
            pl.BlockSpec((bk, n), lambda j, b: (0, 0)),
            part(0), part(1), part(2),
            pl.BlockSpec((2, tn), lambda j, b: (0, j)),
            spectra(0), spectra(1),
        ],
        out_specs=pl.BlockSpec((seq, tn), lambda j, b: (b, j)),
        out_shape=jax.ShapeDtypeStruct((batch * seq, d), BF16),
        scratch_shapes=[pltpu.VMEM((nb, n, tn), F32), pltpu.VMEM((nb, n, tn), BF16),
                        pltpu.VMEM((seq, tn), BF16)],
        compiler_params=_params("parallel", "arbitrary"),
    )(fb, gb, vxx, vxx, vxx, skip, spec, spec)


def kernel(x, norm_mix_g, norm_ffn_g, a_w_in, a_conv_w, a_w_out, b_w_qkv, b_q_norm_g, b_k_norm_g,
           b_rpb, b_w_out, c_w_in, c_short_w, c_short_b, c_f_w1, c_f_b1, c_f_w2, c_f_b2, c_f_w3,
           c_f_freq, c_f_skip, c_w_out, f_w13, f_w2):
    batch, seq, d = x.shape
    depth = norm_mix_g.shape[0]
    heads = d // NA_HEAD_DIM
    h = x.reshape(batch * seq, d)
    bf = lambda w: w.astype(BF16)
    a_w_in, a_w_out, f_w13, f_w2 = bf(a_w_in), bf(a_w_out), bf(f_w13), bf(f_w2)
    ia = ib = ic = 0
    for i in range(depth):
        g_mix = norm_mix_g[i][None, :]
        out_proj = ()
        kind = i % 3
        if kind == 0:
            h = _sconv_mixer(h, g_mix, a_w_in, a_conv_w[ia], a_w_out, ia, seq)
            ia += 1
        elif kind == 1:
            head_gain = jnp.concatenate([
                jnp.tile(b_q_norm_g[ib], heads) * (NA_HEAD_DIM ** -0.5),
                jnp.tile(b_k_norm_g[ib], heads)])[None, :]
            qkv = _qkv(h, g_mix, bf(b_w_qkv[ib]), head_gain)
            h = _natten(qkv.reshape(batch, seq, 3 * d), _bias_table(b_rpb[ib]), bf(b_w_out[ib]),
                        h.reshape(batch, seq, d)).reshape(batch * seq, d)
            ib += 1
        else:
            fl, fr, gb = _block_dft_matrices(seq // HYENA_CONV_BLOCKS)
            filt = _hyena_filters_time(seq, d, c_f_w1[ic], c_f_b1[ic], c_f_w2[ic], c_f_b2[ic],
                                       c_f_w3[ic], c_f_freq[ic])
            spec = _filter_spectra(fl, fr, filt)
            vxx = _proj_conv(h, g_mix, bf(c_w_in[ic]), c_short_w[ic], c_short_b[ic][None, :], seq)
            z = _hyena_conv(fl, gb, vxx, c_f_skip[ic], spec, batch, seq, d)
            out_proj = (z, bf(c_w_out[ic]))
            ic += 1
        h = _ffn(h, norm_ffn_g[i][None, :], f_w13, f_w2, i, *out_proj)
    return h.reshape(batch, seq, d)
```

```python
import functools
import math

import jax
import jax.numpy as jnp
from jax import lax
from jax.experimental import pallas as pl
from jax.experimental.pallas import tpu as pltpu

F32 = jnp.float32
BF16 = jnp.bfloat16
HIGHEST = lax.Precision.HIGHEST

RMS_EPS = 1e-6
GRID_W = 64
NA_HEAD_DIM = 64
NA_WIN_ROWS = 8
NA_WIN_COLS = 16
NA_RPB_ROWS = 2 * NA_WIN_ROWS - 1
NA_RPB_COLS = 2 * NA_WIN_COLS - 1
NA_VALUES_LAG = 3
HYENA_BANDS = 16
HYENA_DECAY_TARGET = 1e-2
HYENA_FAST_DECAY = 0.3
HYENA_SLOW_DECAY = 1.5
HYENA_CONV_BLOCKS = 4
MASK_VALUE = -1e30

V7X_LANES = 128
V7X_SUBLANES = 8
V7X_MXU_DIM = 256
VMEM_LIMIT_BYTES = 56 * 1024 * 1024


def _params(*semantics):
    return pltpu.CompilerParams(dimension_semantics=semantics,
                                vmem_limit_bytes=VMEM_LIMIT_BYTES)


def _rmsnorm_bf16(x, g):
    ms = jnp.mean(x * x, axis=-1, keepdims=True)
    return (x * lax.rsqrt(ms + RMS_EPS) * g).astype(BF16)


def _dot(a, b):
    return jnp.dot(a, b, preferred_element_type=F32)


def _qkv_kernel(x_ref, g_ref, w_ref, hg_ref, o_ref):
    xn = _rmsnorm_bf16(x_ref[...], g_ref[...])
    nc = V7X_MXU_DIM
    qk_cols = hg_ref.shape[1]
    r = lax.broadcasted_iota(jnp.int32, (nc, nc), 0) // NA_HEAD_DIM
    c = lax.broadcasted_iota(jnp.int32, (nc, nc), 1) // NA_HEAD_DIM
    head_ones = jnp.where(r == c, 1.0, 0.0).astype(BF16)
    chunks = list(range(0, w_ref.shape[1], nc))

    def head_sumsq(n, acc):
        return _dot((acc * acc).astype(BF16), head_ones) if n < qk_cols else None

    def finish(n, acc, ssq):
        if ssq is not None:
            acc = acc * lax.rsqrt(ssq * (1.0 / NA_HEAD_DIM) + RMS_EPS) * hg_ref[:, n:n + nc]
        o_ref[:, n:n + nc] = acc.astype(o_ref.dtype)

    acc_q, ssq_q = {}, {}
    for t in range(len(chunks) + 2):
        if t < len(chunks):
            acc_q[t] = _dot(xn, w_ref[:, chunks[t]:chunks[t] + nc])
        if 1 <= t <= len(chunks):
            ssq_q[t - 1] = head_sumsq(chunks[t - 1], acc_q[t - 1])
        if t >= 2:
            finish(chunks[t - 2], acc_q.pop(t - 2), ssq_q.pop(t - 2))


def _qkv(x, g, w, head_gain, *, tm=512):
    t, d = x.shape
    n = w.shape[1]
    return pl.pallas_call(
        _qkv_kernel,
        grid=(t // tm,),
        in_specs=[
            pl.BlockSpec((tm, d), lambda i: (i, 0)),
            pl.BlockSpec((1, d), lambda i: (0, 0)),
            pl.BlockSpec((d, n), lambda i: (0, 0)),
            pl.BlockSpec(head_gain.shape, lambda i: (0, 0)),
        ],
        out_specs=pl.BlockSpec((tm, n), lambda i: (i, 0)),
        out_shape=jax.ShapeDtypeStruct((t, n), BF16),
        compiler_params=_params("parallel"),
    )(x, g, w, head_gain)


def _ffn_kernel(x_ref, g_ref, w13_ref, w2_ref, *rest, hc):
    o_ref, acc_ref = rest[-2:]
    x = x_ref[...]
    if len(rest) == 4:
        z_ref, wo_ref = rest[:2]
        x = x + _dot(z_ref[...], wo_ref[...])
    xn = _rmsnorm_bf16(x, g_ref[...])
    hidden = w2_ref.shape[0]
    for c in range(0, hidden, hc):
        gate = _dot(xn, w13_ref[:, c:c + hc])
        up = _dot(xn, w13_ref[:, hidden + c:hidden + c + hc])
        act = (gate * jax.nn.sigmoid(gate) * up).astype(BF16)
        part = _dot(act, w2_ref[c:c + hc, :])
        if c == 0:
            acc_ref[...] = part
        else:
            acc_ref[...] += part
    o_ref[...] = x + acc_ref[...]


def _ffn(x, g, w13, w2, layer, z=None, w_out=None, *, tm=512, hc=256):
    t, d = x.shape
    hidden = w2.shape[1]
    once = pl.Buffered(1)
    in_specs = [
        pl.BlockSpec((tm, d), lambda i: (i, 0)),
        pl.BlockSpec((1, d), lambda i: (0, 0)),
        pl.BlockSpec((None, d, 2 * hidden), lambda i: (layer, 0, 0), pipeline_mode=once),
        pl.BlockSpec((None, hidden, d), lambda i: (layer, 0, 0), pipeline_mode=once),
    ]
    args = [x, g, w13, w2]
    if z is not None:
        in_specs += [pl.BlockSpec((tm, d), lambda i: (i, 0)),
                     pl.BlockSpec((d, d), lambda i: (0, 0), pipeline_mode=once)]
        args += [z, w_out]
    return pl.pallas_call(
        functools.partial(_ffn_kernel, hc=hc),
        grid=(t // tm,),
        in_specs=in_specs,
        out_specs=pl.BlockSpec((tm, d), lambda i: (i, 0)),
        out_shape=jax.ShapeDtypeStruct((t, d), F32),
        scratch_shapes=[pltpu.VMEM((tm, d), F32)],
        compiler_params=_params("parallel"),
    )(*args)


def _shift_rows(v, prev_row, next_row):
    tm, n = v.shape
    sub = V7X_SUBLANES
    groups = tm // sub
    v3 = v.reshape(groups, sub, n)
    row = lax.broadcasted_iota(jnp.int32, (1, sub, 1), 1)
    down = pltpu.roll(v3, 1, axis=1)
    up = pltpu.roll(v3, sub - 1, axis=1)
    halo = lambda r: jnp.broadcast_to(r[None], (1, sub, n))
    down_nb = jnp.concatenate([halo(prev_row), down[:groups - 1]], axis=0)
    up_nb = jnp.concatenate([up[1:], halo(next_row)], axis=0)
    v_prev = jnp.where(row == 0, down_nb, down)
    v_next = jnp.where(row == sub - 1, up_nb, up)
    return v_prev.reshape(tm, n), v_next.reshape(tm, n)


def _conv3(v, prev_row, next_row, w):
    v_prev, v_next = _shift_rows(v, prev_row, next_row)
    return w[0:1] * v_prev + w[1:2] * v + w[2:3] * v_next


def _finished_tile_edges(i, tiles_per_seq):
    pos = (i - 1) % tiles_per_seq
    return pos == 0, pos == tiles_per_seq - 1


def _delayed_specs(tm, n_tiles):
    current = lambda i: (jnp.minimum(i, n_tiles - 1), 0)
    finished = lambda i: (jnp.maximum(i - 1, 0), 0)
    const = lambda i: (0, 0)
    return current, finished, const


def _sconv_mixer_kernel(x_ref, xf_ref, g_ref, win_ref, cw_ref, wo_ref, o_ref, b_scr, v_scr, edge_scr,
                        *, tiles_per_seq):
    i = pl.program_id(0)
    tm, d = x_ref.shape

    @pl.when(i == 0)
    def _init():
        b_scr[...] = jnp.zeros_like(b_scr)
        v_scr[...] = jnp.zeros_like(v_scr)
        edge_scr[...] = jnp.zeros_like(edge_scr)

    xn = _rmsnorm_bf16(x_ref[...], g_ref[...])
    b_new = _dot(xn, win_ref[:, 0:d]).astype(BF16)
    v_new = _dot(xn, win_ref[:, d:2 * d]) * _dot(xn, win_ref[:, 2 * d:3 * d])

    first, last = _finished_tile_edges(i, tiles_per_seq)
    v = v_scr[...]
    prev_row = jnp.where(first, 0.0, edge_scr[...])
    next_row = jnp.where(last, 0.0, v_new[0:1, :])
    conv = _conv3(v, prev_row, next_row, cw_ref[...])
    y = (b_scr[...].astype(F32) * conv).astype(BF16)
    o_ref[...] = xf_ref[...] + _dot(y, wo_ref[...])

    edge_scr[...] = v[tm - 1:tm, :]
    b_scr[...] = b_new
    v_scr[...] = v_new


def _sconv_mixer(x, g, w_in, conv_w, w_out, layer, seq, *, tm=512):
    t, d = x.shape
    n_tiles = t // tm
    current, finished, const = _delayed_specs(tm, n_tiles)
    stacked = lambda i: (layer, 0, 0)
    return pl.pallas_call(
        functools.partial(_sconv_mixer_kernel, tiles_per_seq=seq // tm),
        grid=(n_tiles + 1,),
        in_specs=[
            pl.BlockSpec((tm, d), current),
            pl.BlockSpec((tm, d), finished),
            pl.BlockSpec((1, d), const),
            pl.BlockSpec((None, d, 3 * d), stacked),
            pl.BlockSpec((3, d), const),
            pl.BlockSpec((None, d, d), stacked),
        ],
        out_specs=pl.BlockSpec((tm, d), finished),
        out_shape=jax.ShapeDtypeStruct((t, d), F32),
        scratch_shapes=[pltpu.VMEM((tm, d), BF16),
                        pltpu.VMEM((tm, d), F32),
                        pltpu.VMEM((1, d), F32)],
        compiler_params=_params("arbitrary"),
    )(x, x, g, w_in, conv_w, w_out)


def _proj_conv_kernel(x_ref, g_ref, w_ref, cw_ref, cb_ref, o_ref, p_scr, edge_scr, *, tiles_per_seq, nc):
    i = pl.program_id(0)
    tm = x_ref.shape[0]

    @pl.when(i == 0)
    def _init():
        p_scr[...] = jnp.zeros_like(p_scr)
        edge_scr[...] = jnp.zeros_like(edge_scr)

    xn = _rmsnorm_bf16(x_ref[...], g_ref[...])
    first, last = _finished_tile_edges(i, tiles_per_seq)
    for c in range(0, w_ref.shape[1], nc):
        cols = slice(c, c + nc)
        p_new = _dot(xn, w_ref[:, cols])
        p = p_scr[:, cols]
        prev_row = jnp.where(first, 0.0, edge_scr[:, cols])
        next_row = jnp.where(last, 0.0, p_new[0:1, :])
        conv = _conv3(p, prev_row, next_row, cw_ref[:, cols])
        o_ref[:, cols] = (conv + cb_ref[:, cols]).astype(o_ref.dtype)
        edge_scr[:, cols] = p[tm - 1:tm, :]
        p_scr[:, cols] = p_new


def _proj_conv(x, g, w, conv_w, conv_b, seq, *, tm=512, nc=512):
    t, d = x.shape
    n = w.shape[1]
    n_tiles = t // tm
    current, finished, const = _delayed_specs(tm, n_tiles)
    return pl.pallas_call(
        functools.partial(_proj_conv_kernel, tiles_per_seq=seq // tm, nc=nc),
        grid=(n_tiles + 1,),
        in_specs=[
            pl.BlockSpec((tm, d), current),
            pl.BlockSpec((1, d), const),
            pl.BlockSpec((d, n), const),
            pl.BlockSpec((3, n), const),
            pl.BlockSpec((1, n), const),
        ],
        out_specs=pl.BlockSpec((tm, n), finished),
        out_shape=jax.ShapeDtypeStruct((t, n), BF16),
        scratch_shapes=[pltpu.VMEM((tm, n), F32),
                        pltpu.VMEM((1, n), F32)],
        compiler_params=_params("arbitrary"),
    )(x, g, w, conv_w, conv_b)


def _bias_table_kernel(r_ref, o_ref):
    kdim = r_ref.shape[1]
    n = o_ref.shape[1]

    def decode(shape):
        c = lax.broadcasted_iota(jnp.int32, shape, 1)
        qc = c // (2 * GRID_W)
        lane = c % (2 * GRID_W)
        second = lane >= GRID_W
        kc = jnp.where(second, lane - GRID_W, lane)
        start = jnp.clip(qc - NA_WIN_COLS // 2, 0, GRID_W - NA_WIN_COLS)
        valid = (kc >= start) & (kc < start + NA_WIN_COLS)
        idx = kc - qc + (NA_WIN_COLS - 1) + jnp.where(second, NA_RPB_COLS, 0)
        return valid, idx

    valid, idx = decode((kdim, n))
    i = lax.broadcasted_iota(jnp.int32, (kdim, n), 0)
    onehot = jnp.where(valid & (i == idx), 1.0, 0.0)
    tab = jnp.dot(r_ref[...], onehot, preferred_element_type=F32, precision=HIGHEST)
    valid_row, _ = decode((1, n))
    o_ref[...] = jnp.where(valid_row, tab, MASK_VALUE)


def _bias_table(rpb):
    heads = rpb.shape[0]
    pairs = NA_RPB_ROWS - 1
    kdim = 64
    pad = jnp.zeros((pairs, heads, kdim - 2 * NA_RPB_COLS), F32)
    rows = jnp.swapaxes(rpb, 0, 1)
    r = jnp.concatenate([rows[:-1], rows[1:], pad], axis=-1).reshape(pairs * heads, kdim)
    n = GRID_W * 2 * GRID_W
    tab = pl.pallas_call(
        _bias_table_kernel,
        out_shape=jax.ShapeDtypeStruct((pairs * heads, n), F32),
        compiler_params=pltpu.CompilerParams(vmem_limit_bytes=VMEM_LIMIT_BYTES),
    )(r)
    return tab.reshape(pairs, heads * GRID_W, 2 * GRID_W)


def _natten_kernel(q_ref, k_ref, v_ref, tab_ref, wo_ref, x_ref, o_ref, att_scr, *, rows_per_step, n_rows):
    pairs = q_ref.shape[2] // V7X_LANES
    lane = lax.broadcasted_iota(jnp.int32, (GRID_W, V7X_LANES), 1)
    low = lane < NA_HEAD_DIM
    win = NA_WIN_ROWS * GRID_W

    def row_body(rr, carry):
        r = pl.program_id(1) * rows_per_step + rr
        rs = jnp.clip(r - NA_WIN_ROWS // 2, 0, n_rows - NA_WIN_ROWS)
        d0 = rs - r + NA_WIN_ROWS - 1
        q_rows = pl.ds(pl.multiple_of(rr * GRID_W, GRID_W), GRID_W)
        k_rows = pl.ds(pl.multiple_of(rs * GRID_W, GRID_W), win)

        def scores(hp):
            cols = slice(hp * V7X_LANES, (hp + 1) * V7X_LANES)
            q2 = q_ref[0, q_rows, cols]
            zero = jnp.zeros_like(q2)
            qs = jnp.concatenate([jnp.where(low, q2, zero), jnp.where(low, zero, q2)], axis=0)
            return lax.dot_general(qs, k_ref[0, k_rows, cols], (((1,), (1,)), ((), ())),
                                   preferred_element_type=F32)

        def softmax(hp, s):
            rows = slice(hp * 2 * GRID_W, (hp + 1) * 2 * GRID_W)
            bias = jnp.concatenate(
                [tab_ref[pl.ds(d0 + 2 * p, 1), rows, :][0] for p in range(NA_WIN_ROWS // 2)], axis=1)
            s = s + bias
            m = jnp.max(s, axis=-1, keepdims=True)
            e = jnp.exp(s - m)
            return e.astype(BF16), 1.0 / jnp.sum(e, axis=-1, keepdims=True)

        def values(hp, p, linv):
            cols = slice(hp * V7X_LANES, (hp + 1) * V7X_LANES)
            o = _dot(p, v_ref[0, k_rows, cols]) * linv
            att_scr[q_rows, cols] = jnp.where(low, o[:GRID_W], o[GRID_W:]).astype(att_scr.dtype)

        s_q, p_q = {}, {}
        for t in range(pairs + 1 + NA_VALUES_LAG):
            if t < pairs:
                s_q[t] = scores(t)
            if 1 <= t <= pairs:
                p_q[t - 1] = softmax(t - 1, s_q.pop(t - 1))
            if t >= 1 + NA_VALUES_LAG:
                hp = t - 1 - NA_VALUES_LAG
                values(hp, *p_q.pop(hp))
        return carry

    lax.fori_loop(0, rows_per_step, row_body, 0)
    o_ref[0] = x_ref[0] + _dot(att_scr[...], wo_ref[...])


def _natten(qkv, tab, w_out, x, *, rows_per_step=16):
    b, s, d3 = qkv.shape
    d = d3 // 3
    n_rows = s // GRID_W
    qs = rows_per_step * GRID_W
    once = pl.Buffered(1)
    return pl.pallas_call(
        functools.partial(_natten_kernel, rows_per_step=rows_per_step, n_rows=n_rows),
        grid=(b, n_rows // rows_per_step),
        in_specs=[
            pl.BlockSpec((1, qs, d), lambda i, j: (i, j, 0)),
            pl.BlockSpec((1, s, d), lambda i, j: (i, 0, 1)),
            pl.BlockSpec((1, s, d), lambda i, j: (i, 0, 2)),
            pl.BlockSpec(tab.shape, lambda i, j: (0, 0, 0), pipeline_mode=once),
            pl.BlockSpec((d, d), lambda i, j: (0, 0), pipeline_mode=once),
            pl.BlockSpec((1, qs, d), lambda i, j: (i, j, 0)),
        ],
        out_specs=pl.BlockSpec((1, qs, d), lambda i, j: (i, j, 0)),
        out_shape=jax.ShapeDtypeStruct((b, s, d), F32),
        scratch_shapes=[pltpu.VMEM((qs, d), BF16)],
        compiler_params=_params("parallel", "arbitrary"),
    )(qkv, qkv, qkv, tab, w_out, x)


def _block_dft_matrices(bk):
    n = 2 * bk
    row = jnp.arange(n, dtype=jnp.int32)
    keff = jnp.where(row <= bk, row, row - bk)
    quarter = jnp.where(row > bk, n // 4, 0)
    t = jnp.arange(n, dtype=jnp.int32)
    turns = (keff[:, None] * t[None, :] + quarter[:, None]) % n
    full = jnp.cos(turns.astype(F32) * (2.0 * math.pi / n))
    weight = jnp.where((row == 0) | (row == bk), 1.0 / n, 2.0 / n)
    inv = (full[:, :bk] * weight[:, None]).T
    return full[:, :bk].astype(BF16), full[:, bk:].astype(BF16), inv.astype(BF16)


def _filter_kernel(bands_ref, w1t_ref, w1c_ref, w1s_ref, b1_ref, w2_ref, b2_ref, freq_ref,
                   w3f_ref, w3r_ref, delta_ref, h_ref, hid_scr):
    seq = h_ref.shape[0] // 2
    j = lax.broadcasted_iota(jnp.int32, (seq, 1), 0)
    pos_f = j.astype(F32)
    pos_r = jnp.where(j == 0, 0, seq - j).astype(F32)

    def mlp(reverse):
        lane = lax.broadcasted_iota(jnp.int32, (1, seq), 1)
        pos = (jnp.where(lane == 0, 0, seq - lane) if reverse else lane).astype(F32)
        t = pos / (seq - 1.0)
        ang = (2.0 * math.pi) * pos / seq * bands_ref[...]
        pre = (t * w1t_ref[...]
               + jnp.dot(w1c_ref[...], jnp.cos(ang), preferred_element_type=F32, precision=HIGHEST)
               + jnp.dot(w1s_ref[...], -jnp.sin(ang), preferred_element_type=F32, precision=HIGHEST)
               + b1_ref[...])
        hid = jnp.sin(freq_ref[...] * pre)
        hid = jnp.sin(freq_ref[...] * (
            jnp.dot(w2_ref[...], hid, preferred_element_type=F32, precision=HIGHEST) + b2_ref[...]))
        return hid.T

    @pl.when((pl.program_id(0) == 0) & (pl.program_id(1) == 0))
    def _hidden():
        hid_scr[0] = mlp(False)
        hid_scr[1] = mlp(True)

    def filt(hid, w3_ref, pos):
        f = jnp.dot(hid, w3_ref[...], preferred_element_type=F32, precision=HIGHEST)
        return f * jnp.exp(-(pos / (seq - 1.0)) * delta_ref[...])

    fwd = filt(hid_scr[0], w3f_ref, pos_f)
    rev = filt(hid_scr[1], w3r_ref, pos_r)
    top = fwd + jnp.where(j == 0, rev, 0.0)
    bot = jnp.where(j == 0, 0.0, rev)
    l1 = jnp.sum(jnp.abs(top), axis=0, keepdims=True) + jnp.sum(jnp.abs(bot), axis=0, keepdims=True)
    inv = 1.0 / l1
    h_ref[0:seq, :] = (top * inv).astype(h_ref.dtype)
    h_ref[seq:2 * seq, :] = (bot * inv).astype(h_ref.dtype)


def _hyena_filters_time(seq, d, w1, b1, w2, b2, w3, freq, *, tn=512):
    nb = HYENA_BANDS
    hid = w2.shape[0]
    bands = jnp.linspace(1e-4, nb - 1, nb, dtype=F32)[:, None]
    lt = math.log(HYENA_DECAY_TARGET)
    deltas = jnp.abs(jnp.linspace(lt / HYENA_SLOW_DECAY, lt / HYENA_FAST_DECAY, d, dtype=F32))[None, :]
    small = lambda shape: pl.BlockSpec(shape, lambda o, c: (0, 0))
    cpo = d // tn
    return pl.pallas_call(
        _filter_kernel,
        grid=(2, cpo),
        in_specs=[
            small((nb, 1)), small((hid, 1)), small((hid, nb)), small((hid, nb)), small((hid, 1)),
            small((hid, hid)), small((hid, 1)), small((hid, 1)),
            pl.BlockSpec((hid, tn), lambda o, c: (0, 2 * o * cpo + c)),
            pl.BlockSpec((hid, tn), lambda o, c: (0, (2 * o + 1) * cpo + c)),
            pl.BlockSpec((1, tn), lambda o, c: (0, c)),
        ],
        out_specs=pl.BlockSpec((2 * seq, tn), lambda o, c: (0, o * cpo + c)),
        out_shape=jax.ShapeDtypeStruct((2 * seq, 2 * d), BF16),
        scratch_shapes=[pltpu.VMEM((2, seq, hid), F32)],
        compiler_params=_params("arbitrary", "arbitrary"),
    )(bands, w1[0:1].T, w1[1:1 + nb].T, w1[1 + nb:1 + 2 * nb].T, b1[:, None], w2.T, b2[:, None],
      freq[:, None], w3, w3, deltas)


def _filter_spectra_kernel(fl_ref, fr_ref, top_ref, bot_ref, o_ref):
    o_ref[0] = _dot(fl_ref[...], top_ref[...]) + _dot(fr_ref[...], bot_ref[...])


def _filter_spectra(fl, fr, filt, *, tn=1024):
    n, bk = fl.shape
    blocks = filt.shape[0] // bk
    nb = blocks // 2
    cols = filt.shape[1]
    half = pl.BlockSpec((n, bk), lambda e, j: (0, 0))
    return pl.pallas_call(
        _filter_spectra_kernel,
        grid=(2 * nb - 1, cols // tn),
        in_specs=[
            half, half,
            pl.BlockSpec((bk, tn), lambda e, j: ((e + blocks - (nb - 1)) % blocks, j)),
            pl.BlockSpec((bk, tn), lambda e, j: ((e + blocks - nb) % blocks, j)),
        ],
        out_specs=pl.BlockSpec((1, n, tn), lambda e, j: (e, 0, j)),
        out_shape=jax.ShapeDtypeStruct((2 * nb - 1, n, cols), F32),
        compiler_params=_params("parallel", "parallel"),
    )(fl, fr, filt, filt)


def _block_conv_pass(fb_ref, gb_ref, u_ref, gate_ref, skip, h_ref, o_ref, u_scr, y_scr, rc):
    n, bk = fb_ref.shape
    nb = u_ref.shape[0] // bk
    for jb in range(nb):
        u_scr[jb] = _dot(fb_ref[...], u_ref[jb * bk:(jb + 1) * bk, :])
    row0 = lax.broadcasted_iota(jnp.int32, (rc, 1), 0) == 0
    for ib in range(nb):
        for r in range(0, bk, rc):
            re = slice(r, r + rc)
            im = slice(bk + r, bk + r + rc)
            acc_re = acc_im = None
            for jb in range(nb):
                dlt = ib - jb + nb - 1
                ure, uim = u_scr[jb, re, :], u_scr[jb, im, :]
                hre, him = h_ref[dlt, re, :], h_ref[dlt, im, :]
                if r == 0:
                    t_re = ure * hre - uim * jnp.where(row0, 0.0, him)
                    t_im = jnp.where(row0, 0.0, ure) * him + uim * jnp.where(row0, him, hre)
                else:
                    t_re = ure * hre - uim * him
                    t_im = ure * him + uim * hre
                acc_re = t_re if acc_re is None else acc_re + t_re
                acc_im = t_im if acc_im is None else acc_im + t_im
            y_scr[ib, re, :] = acc_re.astype(y_scr.dtype)
            y_scr[ib, im, :] = acc_im.astype(y_scr.dtype)
        y = _dot(gb_ref[...], y_scr[ib])
        rows = slice(ib * bk, (ib + 1) * bk)
        u = u_ref[rows, :].astype(F32)
        o_ref[rows, :] = (gate_ref[rows, :].astype(F32) * (y + u * skip)).astype(o_ref.dtype)


def _hyena_conv_kernel(fb_ref, gb_ref, v_ref, x1_ref, x2_ref, skip_ref, h1_ref, h2_ref, o_ref,
                       u_scr, y_scr, z_scr, *, rc):
    _block_conv_pass(fb_ref, gb_ref, v_ref, x1_ref, skip_ref[0:1, :], h1_ref, z_scr, u_scr, y_scr, rc)
    _block_conv_pass(fb_ref, gb_ref, z_scr, x2_ref, skip_ref[1:2, :], h2_ref, o_ref, u_scr, y_scr, rc)


def _hyena_conv(fb, gb, vxx, skip, spec, batch, seq, d, *, tn=256, rc=32):
    n, bk = fb.shape
    nb = seq // bk
    cb = d // tn
    part = lambda c: pl.BlockSpec((seq, tn), lambda j, b: (b, c * cb + j))
    spectra = lambda order: pl.BlockSpec((2 * nb - 1, n, tn), lambda j, b: (0, 0, order * cb + j))
    return pl.pallas_call(
        functools.partial(_hyena_conv_kernel, rc=rc),
        grid=(cb, batch),
        in_specs=[
            pl.BlockSpec((n, bk), lambda j, b: (0, 0)),
            pl.BlockSpec((bk, n), lambda j, b: (0, 0)),
            part(0), part(1), part(2),
            pl.BlockSpec((2, tn), lambda j, b: (0, j)),
            spectra(0), spectra(1),
        ],
        out_specs=pl.BlockSpec((seq, tn), lambda j, b: (b, j)),
        out_shape=jax.ShapeDtypeStruct((batch * seq, d), BF16),
        scratch_shapes=[pltpu.VMEM((nb, n, tn), F32), pltpu.VMEM((nb, n, tn), BF16),
                        pltpu.VMEM((seq, tn), BF16)],
        compiler_params=_params("parallel", "arbitrary"),
    )(fb, gb, vxx, vxx, vxx, skip, spec, spec)


def kernel(x, norm_mix_g, norm_ffn_g, a_w_in, a_conv_w, a_w_out, b_w_qkv, b_q_norm_g, b_k_norm_g,
           b_rpb, b_w_out, c_w_in, c_short_w, c_short_b, c_f_w1, c_f_b1, c_f_w2, c_f_b2, c_f_w3,
           c_f_freq, c_f_skip, c_w_out, f_w13, f_w2):
    batch, seq, d = x.shape
    depth = norm_mix_g.shape[0]
    heads = d // NA_HEAD_DIM
    h = x.reshape(batch * seq, d)
    bf = lambda w: w.astype(BF16)
    a_w_in, a_w_out, f_w13, f_w2 = bf(a_w_in), bf(a_w_out), bf(f_w13), bf(f_w2)
    ia = ib = ic = 0
    for i in range(depth):
        g_mix = norm_mix_g[i][None, :]
        out_proj = ()
        kind = i % 3
        if kind == 0:
            h = _sconv_mixer(h, g_mix, a_w_in, a_conv_w[ia], a_w_out, ia, seq)
            ia += 1
        elif kind == 1:
            head_gain = jnp.concatenate([
                jnp.tile(b_q_norm_g[ib], heads) * (NA_HEAD_DIM ** -0.5),
                jnp.tile(b_k_norm_g[ib], heads)])[None, :]
            qkv = _qkv(h, g_mix, bf(b_w_qkv[ib]), head_gain)
            h = _natten(qkv.reshape(batch, seq, 3 * d), _bias_table(b_rpb[ib]), bf(b_w_out[ib]),
                        h.reshape(batch, seq, d)).reshape(batch * seq, d)
            ib += 1
        else:
            fl, fr, gb = _block_dft_matrices(seq // HYENA_CONV_BLOCKS)
            filt = _hyena_filters_time(seq, d, c_f_w1[ic], c_f_b1[ic], c_f_w2[ic], c_f_b2[ic],
                                       c_f_w3[ic], c_f_freq[ic])
            spec = _filter_spectra(fl, fr, filt)
            vxx = _proj_conv(h, g_mix, bf(c_w_in[ic]), c_short_w[ic], c_short_b[ic][None, :], seq)
            z = _hyena_conv(fl, gb, vxx, c_f_skip[ic], spec, batch, seq, d)
            out_proj = (z, bf(c_w_out[ic]))
            ic += 1
        h = _ffn(h, norm_ffn_g[i][None, :], f_w13, f_w2, i, *out_proj)
    return h.reshape(batch, seq, d)
```

```python
import functools
import math

import jax
import jax.numpy as jnp
from jax import lax
from jax.experimental import pallas as pl
from jax.experimental.pallas import tpu as pltpu

F32 = jnp.float32
BF16 = jnp.bfloat16
HIGHEST = lax.Precision.HIGHEST

RMS_EPS = 1e-6
GRID_W = 64
NA_HEAD_DIM = 64
NA_WIN_ROWS = 8
NA_WIN_COLS = 16
NA_RPB_ROWS = 2 * NA_WIN_ROWS - 1
NA_RPB_COLS = 2 * NA_WIN_COLS - 1
NA_VALUES_LAG = 3
HYENA_BANDS = 16
HYENA_DECAY_TARGET = 1e-2
HYENA_FAST_DECAY = 0.3
HYENA_SLOW_DECAY = 1.5
HYENA_CONV_BLOCKS = 4
MASK_VALUE = -1e30

V7X_LANES = 128
V7X_SUBLANES = 8
V7X_MXU_DIM = 256
VMEM_LIMIT_BYTES = 56 * 1024 * 1024


def _params(*semantics):
    return pltpu.CompilerParams(dimension_semantics=semantics,
                                vmem_limit_bytes=VMEM_LIMIT_BYTES)


def _rmsnorm_bf16(x, g):
    ms = jnp.mean(x * x, axis=-1, keepdims=True)
    return (x * lax.rsqrt(ms + RMS_EPS) * g).astype(BF16)


def _dot(a, b):
    return jnp.dot(a, b, preferred_element_type=F32)


def _qkv_kernel(x_ref, g_ref, w_ref, hg_ref, o_ref):
    xn = _rmsnorm_bf16(x_ref[...], g_ref[...])
    nc = V7X_MXU_DIM
    qk_cols = hg_ref.shape[1]
    r = lax.broadcasted_iota(jnp.int32, (nc, nc), 0) // NA_HEAD_DIM
    c = lax.broadcasted_iota(jnp.int32, (nc, nc), 1) // NA_HEAD_DIM
    head_ones = jnp.where(r == c, 1.0, 0.0).astype(BF16)
    chunks = list(range(0, w_ref.shape[1], nc))

    def head_sumsq(n, acc):
        return _dot((acc * acc).astype(BF16), head_ones) if n < qk_cols else None

    def finish(n, acc, ssq):
        if ssq is not None:
            acc = acc * lax.rsqrt(ssq * (1.0 / NA_HEAD_DIM) + RMS_EPS) * hg_ref[:, n:n + nc]
        o_ref[:, n:n + nc] = acc.astype(o_ref.dtype)

    acc_q, ssq_q = {}, {}
    for t in range(len(chunks) + 2):
        if t < len(chunks):
            acc_q[t] = _dot(xn, w_ref[:, chunks[t]:chunks[t] + nc])
        if 1 <= t <= len(chunks):
            ssq_q[t - 1] = head_sumsq(chunks[t - 1], acc_q[t - 1])
        if t >= 2:
            finish(chunks[t - 2], acc_q.pop(t - 2), ssq_q.pop(t - 2))


def _qkv(x, g, w, head_gain, *, tm=512):
    t, d = x.shape
    n = w.shape[1]
    return pl.pallas_call(
        _qkv_kernel,
        grid=(t // tm,),
        in_specs=[
            pl.BlockSpec((tm, d), lambda i: (i, 0)),
            pl.BlockSpec((1, d), lambda i: (0, 0)),
            pl.BlockSpec((d, n), lambda i: (0, 0)),
            pl.BlockSpec(head_gain.shape, lambda i: (0, 0)),
        ],
        out_specs=pl.BlockSpec((tm, n), lambda i: (i, 0)),
        out_shape=jax.ShapeDtypeStruct((t, n), BF16),
        compiler_params=_params("parallel"),
    )(x, g, w, head_gain)


def _swiglu_chunks(xn, w13_ref, w2_ref, acc_ref, hc, gate_hook=None):
    hidden = w2_ref.shape[0]
    for ci, c in enumerate(range(0, hidden, hc)):
        gate = _dot(xn, w13_ref[:, c:c + hc])
        up = _dot(xn, w13_ref[:, hidden + c:hidden + c + hc])
        if gate_hook is not None:
            gate = gate_hook(ci, gate)
        act = (gate * jax.nn.sigmoid(gate) * up).astype(BF16)
        part = _dot(act, w2_ref[c:c + hc, :])
        if c == 0:
            acc_ref[...] = part
        else:
            acc_ref[...] += part


def _ffn_kernel(x_ref, xnext_ref, g_ref, w13_ref, w2_ref, o_ref, acc_ref, xn_scr, xnext_scr, *, hc, rs):
    tm = x_ref.shape[0]

    @pl.when(pl.program_id(0) == 0)
    def _first():
        xn_scr[...] = _rmsnorm_bf16(x_ref[...], g_ref[...])

    def normalise_next_slice(ci, gate):
        if not 1 <= ci <= tm // rs:
            return gate
        rows = slice((ci - 1) * rs, ci * rs)
        piece = _rmsnorm_bf16(xnext_ref[rows, :], g_ref[...])
        xnext_scr[rows, :] = piece
        bits = pltpu.bitcast(piece[:, :hc], jnp.uint32)
        zero = ((bits >> 16) >> 16).astype(F32)
        return gate + jnp.concatenate([zero, jnp.zeros((tm - rs // 2, hc), F32)], axis=0)

    _swiglu_chunks(xn_scr[...], w13_ref, w2_ref, acc_ref, hc, normalise_next_slice)
    o_ref[...] = x_ref[...] + acc_ref[...]
    xn_scr[...] = xnext_scr[...]


def _proj_ffn_kernel(x_ref, z_ref, wo_ref, g_ref, w13_ref, w2_ref, o_ref, acc_ref, *, hc):
    x = x_ref[...] + _dot(z_ref[...], wo_ref[...])
    _swiglu_chunks(_rmsnorm_bf16(x, g_ref[...]), w13_ref, w2_ref, acc_ref, hc)
    o_ref[...] = x + acc_ref[...]


def _ffn(x, g, w13, w2, layer, z=None, w_out=None, *, tm=512, hc=256, rs=64):
    t, d = x.shape
    hidden = w2.shape[1]
    n_tiles = t // tm
    assert hidden // hc > tm // rs
    once = pl.Buffered(1)
    rows = pl.BlockSpec((tm, d), lambda i: (i, 0))
    weights = [
        pl.BlockSpec((1, d), lambda i: (0, 0)),
        pl.BlockSpec((None, d, 2 * hidden), lambda i: (layer, 0, 0), pipeline_mode=once),
        pl.BlockSpec((None, hidden, d), lambda i: (layer, 0, 0), pipeline_mode=once),
    ]
    scratch = [pltpu.VMEM((tm, d), F32)]
    if z is None:
        body = functools.partial(_ffn_kernel, hc=hc, rs=rs)
        next_rows = pl.BlockSpec((tm, d), lambda i: (jnp.minimum(i + 1, n_tiles - 1), 0))
        in_specs, args = [rows, next_rows] + weights, [x, x, g, w13, w2]
        scratch += [pltpu.VMEM((tm, d), BF16), pltpu.VMEM((tm, d), BF16)]
    else:
        body = functools.partial(_proj_ffn_kernel, hc=hc)
        in_specs = [rows, rows, pl.BlockSpec((d, d), lambda i: (0, 0), pipeline_mode=once)] + weights
        args = [x, z, w_out, g, w13, w2]
    return pl.pallas_call(
        body,
        grid=(n_tiles,),
        in_specs=in_specs,
        out_specs=rows,
        out_shape=jax.ShapeDtypeStruct((t, d), F32),
        scratch_shapes=scratch,
        compiler_params=_params("arbitrary"),
    )(*args)


def _shift_rows(v, prev_row, next_row):
    tm, n = v.shape
    sub = V7X_SUBLANES
    groups = tm // sub
    v3 = v.reshape(groups, sub, n)
    row = lax.broadcasted_iota(jnp.int32, (1, sub, 1), 1)
    down = pltpu.roll(v3, 1, axis=1)
    up = pltpu.roll(v3, sub - 1, axis=1)
    halo = lambda r: jnp.broadcast_to(r[None], (1, sub, n))
    down_nb = jnp.concatenate([halo(prev_row), down[:groups - 1]], axis=0)
    up_nb = jnp.concatenate([up[1:], halo(next_row)], axis=0)
    v_prev = jnp.where(row == 0, down_nb, down)
    v_next = jnp.where(row == sub - 1, up_nb, up)
    return v_prev.reshape(tm, n), v_next.reshape(tm, n)


def _conv3(v, prev_row, next_row, w):
    v_prev, v_next = _shift_rows(v, prev_row, next_row)
    return w[0:1] * v_prev + w[1:2] * v + w[2:3] * v_next


def _finished_tile_edges(i, tiles_per_seq):
    pos = (i - 1) % tiles_per_seq
    return pos == 0, pos == tiles_per_seq - 1


def _delayed_specs(tm, n_tiles):
    current = lambda i: (jnp.minimum(i, n_tiles - 1), 0)
    finished = lambda i: (jnp.maximum(i - 1, 0), 0)
    const = lambda i: (0, 0)
    return current, finished, const


def _sconv_mixer_kernel(x_ref, xf_ref, g_ref, win_ref, cw_ref, wo_ref, o_ref, b_scr, v_scr, edge_scr,
                        *, tiles_per_seq):
    i = pl.program_id(0)
    tm, d = x_ref.shape

    @pl.when(i == 0)
    def _init():
        b_scr[...] = jnp.zeros_like(b_scr)
        v_scr[...] = jnp.zeros_like(v_scr)
        edge_scr[...] = jnp.zeros_like(edge_scr)

    xn = _rmsnorm_bf16(x_ref[...], g_ref[...])
    b_new = _dot(xn, win_ref[:, 0:d]).astype(BF16)
    v_new = _dot(xn, win_ref[:, d:2 * d]) * _dot(xn, win_ref[:, 2 * d:3 * d])

    first, last = _finished_tile_edges(i, tiles_per_seq)
    v = v_scr[...]
    prev_row = jnp.where(first, 0.0, edge_scr[...])
    next_row = jnp.where(last, 0.0, v_new[0:1, :])
    conv = _conv3(v, prev_row, next_row, cw_ref[...])
    y = (b_scr[...].astype(F32) * conv).astype(BF16)
    o_ref[...] = xf_ref[...] + _dot(y, wo_ref[...])

    edge_scr[...] = v[tm - 1:tm, :]
    b_scr[...] = b_new
    v_scr[...] = v_new


def _sconv_mixer(x, g, w_in, conv_w, w_out, layer, seq, *, tm=512):
    t, d = x.shape
    n_tiles = t // tm
    current, finished, const = _delayed_specs(tm, n_tiles)
    stacked = lambda i: (layer, 0, 0)
    return pl.pallas_call(
        functools.partial(_sconv_mixer_kernel, tiles_per_seq=seq // tm),
        grid=(n_tiles + 1,),
        in_specs=[
            pl.BlockSpec((tm, d), current),
            pl.BlockSpec((tm, d), finished),
            pl.BlockSpec((1, d), const),
            pl.BlockSpec((None, d, 3 * d), stacked),
            pl.BlockSpec((3, d), const),
            pl.BlockSpec((None, d, d), stacked),
        ],
        out_specs=pl.BlockSpec((tm, d), finished),
        out_shape=jax.ShapeDtypeStruct((t, d), F32),
        scratch_shapes=[pltpu.VMEM((tm, d), BF16),
                        pltpu.VMEM((tm, d), F32),
                        pltpu.VMEM((1, d), F32)],
        compiler_params=_params("arbitrary"),
    )(x, x, g, w_in, conv_w, w_out)


def _proj_conv_kernel(x_ref, g_ref, w_ref, cw_ref, cb_ref, o_ref, p_scr, edge_scr, *, tiles_per_seq, nc):
    i = pl.program_id(0)
    tm = x_ref.shape[0]

    @pl.when(i == 0)
    def _init():
        p_scr[...] = jnp.zeros_like(p_scr)
        edge_scr[...] = jnp.zeros_like(edge_scr)

    xn = _rmsnorm_bf16(x_ref[...], g_ref[...])
    first, last = _finished_tile_edges(i, tiles_per_seq)
    for c in range(0, w_ref.shape[1], nc):
        cols = slice(c, c + nc)
        p_new = _dot(xn, w_ref[:, cols])
        p = p_scr[:, cols]
        prev_row = jnp.where(first, 0.0, edge_scr[:, cols])
        next_row = jnp.where(last, 0.0, p_new[0:1, :])
        conv = _conv3(p, prev_row, next_row, cw_ref[:, cols])
        o_ref[:, cols] = (conv + cb_ref[:, cols]).astype(o_ref.dtype)
        edge_scr[:, cols] = p[tm - 1:tm, :]
        p_scr[:, cols] = p_new


def _proj_conv(x, g, w, conv_w, conv_b, seq, *, tm=512, nc=512):
    t, d = x.shape
    n = w.shape[1]
    n_tiles = t // tm
    current, finished, const = _delayed_specs(tm, n_tiles)
    return pl.pallas_call(
        functools.partial(_proj_conv_kernel, tiles_per_seq=seq // tm, nc=nc),
        grid=(n_tiles + 1,),
        in_specs=[
            pl.BlockSpec((tm, d), current),
            pl.BlockSpec((1, d), const),
            pl.BlockSpec((d, n), const),
            pl.BlockSpec((3, n), const),
            pl.BlockSpec((1, n), const),
        ],
        out_specs=pl.BlockSpec((tm, n), finished),
        out_shape=jax.ShapeDtypeStruct((t, n), BF16),
        scratch_shapes=[pltpu.VMEM((tm, n), F32),
                        pltpu.VMEM((1, n), F32)],
        compiler_params=_params("arbitrary"),
    )(x, g, w, conv_w, conv_b)


def _bias_table_kernel(r_ref, o_ref):
    kdim = r_ref.shape[1]
    n = o_ref.shape[1]

    def decode(shape):
        c = lax.broadcasted_iota(jnp.int32, shape, 1)
        qc = c // (2 * GRID_W)
        lane = c % (2 * GRID_W)
        second = lane >= GRID_W
        kc = jnp.where(second, lane - GRID_W, lane)
        start = jnp.clip(qc - NA_WIN_COLS // 2, 0, GRID_W - NA_WIN_COLS)
        valid = (kc >= start) & (kc < start + NA_WIN_COLS)
        idx = kc - qc + (NA_WIN_COLS - 1) + jnp.where(second, NA_RPB_COLS, 0)
        return valid, idx

    valid, idx = decode((kdim, n))
    i = lax.broadcasted_iota(jnp.int32, (kdim, n), 0)
    onehot = jnp.where(valid & (i == idx), 1.0, 0.0)
    tab = jnp.dot(r_ref[...], onehot, preferred_element_type=F32, precision=HIGHEST)
    valid_row, _ = decode((1, n))
    o_ref[...] = jnp.where(valid_row, tab, MASK_VALUE)


def _bias_table(rpb):
    heads = rpb.shape[0]
    pairs = NA_RPB_ROWS - 1
    kdim = 64
    pad = jnp.zeros((pairs, heads, kdim - 2 * NA_RPB_COLS), F32)
    rows = jnp.swapaxes(rpb, 0, 1)
    r = jnp.concatenate([rows[:-1], rows[1:], pad], axis=-1).reshape(pairs * heads, kdim)
    n = GRID_W * 2 * GRID_W
    tab = pl.pallas_call(
        _bias_table_kernel,
        out_shape=jax.ShapeDtypeStruct((pairs * heads, n), F32),
        compiler_params=pltpu.CompilerParams(vmem_limit_bytes=VMEM_LIMIT_BYTES),
    )(r)
    return tab.reshape(pairs, heads * GRID_W, 2 * GRID_W)


def _natten_kernel(q_ref, k_ref, v_ref, tab_ref, wo_ref, x_ref, o_ref, att_scr, *, rows_per_step, n_rows):
    pairs = q_ref.shape[2] // V7X_LANES
    lane = lax.broadcasted_iota(jnp.int32, (GRID_W, V7X_LANES), 1)
    low = lane < NA_HEAD_DIM
    win = NA_WIN_ROWS * GRID_W

    def row_body(rr, carry):
        r = pl.program_id(1) * rows_per_step + rr
        rs = jnp.clip(r - NA_WIN_ROWS // 2, 0, n_rows - NA_WIN_ROWS)
        d0 = rs - r + NA_WIN_ROWS - 1
        q_rows = pl.ds(pl.multiple_of(rr * GRID_W, GRID_W), GRID_W)
        k_rows = pl.ds(pl.multiple_of(rs * GRID_W, GRID_W), win)

        def scores(hp):
            cols = slice(hp * V7X_LANES, (hp + 1) * V7X_LANES)
            q2 = q_ref[0, q_rows, cols]
            zero = jnp.zeros_like(q2)
            qs = jnp.concatenate([jnp.where(low, q2, zero), jnp.where(low, zero, q2)], axis=0)
            return lax.dot_general(qs, k_ref[0, k_rows, cols], (((1,), (1,)), ((), ())),
                                   preferred_element_type=F32)

        def softmax(hp, s):
            rows = slice(hp * 2 * GRID_W, (hp + 1) * 2 * GRID_W)
            bias = jnp.concatenate(
                [tab_ref[pl.ds(d0 + 2 * p, 1), rows, :][0] for p in range(NA_WIN_ROWS // 2)], axis=1)
            s = s + bias
            m = jnp.max(s, axis=-1, keepdims=True)
            e = jnp.exp(s - m)
            return e.astype(BF16), 1.0 / jnp.sum(e, axis=-1, keepdims=True)

        def values(hp, p, linv):
            cols = slice(hp * V7X_LANES, (hp + 1) * V7X_LANES)
            o = _dot(p, v_ref[0, k_rows, cols]) * linv
            att_scr[q_rows, cols] = jnp.where(low, o[:GRID_W], o[GRID_W:]).astype(att_scr.dtype)

        s_q, p_q = {}, {}
        for t in range(pairs + 1 + NA_VALUES_LAG):
            if t < pairs:
                s_q[t] = scores(t)
            if 1 <= t <= pairs:
                p_q[t - 1] = softmax(t - 1, s_q.pop(t - 1))
            if t >= 1 + NA_VALUES_LAG:
                hp = t - 1 - NA_VALUES_LAG
                values(hp, *p_q.pop(hp))
        return carry

    lax.fori_loop(0, rows_per_step, row_body, 0)
    o_ref[0] = x_ref[0] + _dot(att_scr[...], wo_ref[...])


def _natten(qkv, tab, w_out, x, *, rows_per_step=16):
    b, s, d3 = qkv.shape
    d = d3 // 3
    n_rows = s // GRID_W
    qs = rows_per_step * GRID_W
    once = pl.Buffered(1)
    return pl.pallas_call(
        functools.partial(_natten_kernel, rows_per_step=rows_per_step, n_rows=n_rows),
        grid=(b, n_rows // rows_per_step),
        in_specs=[
            pl.BlockSpec((1, qs, d), lambda i, j: (i, j, 0)),
            pl.BlockSpec((1, s, d), lambda i, j: (i, 0, 1)),
            pl.BlockSpec((1, s, d), lambda i, j: (i, 0, 2)),
            pl.BlockSpec(tab.shape, lambda i, j: (0, 0, 0), pipeline_mode=once),
            pl.BlockSpec((d, d), lambda i, j: (0, 0), pipeline_mode=once),
            pl.BlockSpec((1, qs, d), lambda i, j: (i, j, 0)),
        ],
        out_specs=pl.BlockSpec((1, qs, d), lambda i, j: (i, j, 0)),
        out_shape=jax.ShapeDtypeStruct((b, s, d), F32),
        scratch_shapes=[pltpu.VMEM((qs, d), BF16)],
        compiler_params=_params("parallel", "arbitrary"),
    )(qkv, qkv, qkv, tab, w_out, x)


def _block_dft_matrices(bk):
    n = 2 * bk
    row = jnp.arange(n, dtype=jnp.int32)
    keff = jnp.where(row <= bk, row, row - bk)
    quarter = jnp.where(row > bk, n // 4, 0)
    t = jnp.arange(n, dtype=jnp.int32)
    turns = (keff[:, None] * t[None, :] + quarter[:, None]) % n
    full = jnp.cos(turns.astype(F32) * (2.0 * math.pi / n))
    weight = jnp.where((row == 0) | (row == bk), 1.0 / n, 2.0 / n)
    inv = (full[:, :bk] * weight[:, None]).T
    return full[:, :bk].astype(BF16), full[:, bk:].astype(BF16), inv.astype(BF16)


def _filter_kernel(bands_ref, w1t_ref, w1c_ref, w1s_ref, b1_ref, w2_ref, b2_ref, freq_ref,
                   w3f_ref, w3r_ref, delta_ref, h_ref, hid_scr):
    seq = h_ref.shape[0] // 2
    j = lax.broadcasted_iota(jnp.int32, (seq, 1), 0)
    pos_f = j.astype(F32)
    pos_r = jnp.where(j == 0, 0, seq - j).astype(F32)

    def mlp(reverse):
        lane = lax.broadcasted_iota(jnp.int32, (1, seq), 1)
        pos = (jnp.where(lane == 0, 0, seq - lane) if reverse else lane).astype(F32)
        t = pos / (seq - 1.0)
        ang = (2.0 * math.pi) * pos / seq * bands_ref[...]
        pre = (t * w1t_ref[...]
               + jnp.dot(w1c_ref[...], jnp.cos(ang), preferred_element_type=F32, precision=HIGHEST)
               + jnp.dot(w1s_ref[...], -jnp.sin(ang), preferred_element_type=F32, precision=HIGHEST)
               + b1_ref[...])
        hid = jnp.sin(freq_ref[...] * pre)
        hid = jnp.sin(freq_ref[...] * (
            jnp.dot(w2_ref[...], hid, preferred_element_type=F32, precision=HIGHEST) + b2_ref[...]))
        return hid.T

    @pl.when((pl.program_id(0) == 0) & (pl.program_id(1) == 0))
    def _hidden():
        hid_scr[0] = mlp(False)
        hid_scr[1] = mlp(True)

    def filt(hid, w3_ref, pos):
        f = jnp.dot(hid, w3_ref[...], preferred_element_type=F32, precision=HIGHEST)
        return f * jnp.exp(-(pos / (seq - 1.0)) * delta_ref[...])

    fwd = filt(hid_scr[0], w3f_ref, pos_f)
    rev = filt(hid_scr[1], w3r_ref, pos_r)
    top = fwd + jnp.where(j == 0, rev, 0.0)
    bot = jnp.where(j == 0, 0.0, rev)
    l1 = jnp.sum(jnp.abs(top), axis=0, keepdims=True) + jnp.sum(jnp.abs(bot), axis=0, keepdims=True)
    inv = 1.0 / l1
    h_ref[0:seq, :] = (top * inv).astype(h_ref.dtype)
    h_ref[seq:2 * seq, :] = (bot * inv).astype(h_ref.dtype)


def _hyena_filters_time(seq, d, w1, b1, w2, b2, w3, freq, *, tn=512):
    nb = HYENA_BANDS
    hid = w2.shape[0]
    bands = jnp.linspace(1e-4, nb - 1, nb, dtype=F32)[:, None]
    lt = math.log(HYENA_DECAY_TARGET)
    deltas = jnp.abs(jnp.linspace(lt / HYENA_SLOW_DECAY, lt / HYENA_FAST_DECAY, d, dtype=F32))[None, :]
    small = lambda shape: pl.BlockSpec(shape, lambda o, c: (0, 0))
    cpo = d // tn
    return pl.pallas_call(
        _filter_kernel,
        grid=(2, cpo),
        in_specs=[
            small((nb, 1)), small((hid, 1)), small((hid, nb)), small((hid, nb)), small((hid, 1)),
            small((hid, hid)), small((hid, 1)), small((hid, 1)),
            pl.BlockSpec((hid, tn), lambda o, c: (0, 2 * o * cpo + c)),
            pl.BlockSpec((hid, tn), lambda o, c: (0, (2 * o + 1) * cpo + c)),
            pl.BlockSpec((1, tn), lambda o, c: (0, c)),
        ],
        out_specs=pl.BlockSpec((2 * seq, tn), lambda o, c: (0, o * cpo + c)),
        out_shape=jax.ShapeDtypeStruct((2 * seq, 2 * d), BF16),
        scratch_shapes=[pltpu.VMEM((2, seq, hid), F32)],
        compiler_params=_params("arbitrary", "arbitrary"),
    )(bands, w1[0:1].T, w1[1:1 + nb].T, w1[1 + nb:1 + 2 * nb].T, b1[:, None], w2.T, b2[:, None],
      freq[:, None], w3, w3, deltas)


def _filter_spectra_kernel(fl_ref, fr_ref, top_ref, bot_ref, o_ref):
    o_ref[0] = _dot(fl_ref[...], top_ref[...]) + _dot(fr_ref[...], bot_ref[...])


def _filter_spectra(fl, fr, filt, *, tn=1024):
    n, bk = fl.shape
    blocks = filt.shape[0] // bk
    nb = blocks // 2
    cols = filt.shape[1]
    half = pl.BlockSpec((n, bk), lambda e, j: (0, 0))
    return pl.pallas_call(
        _filter_spectra_kernel,
        grid=(2 * nb - 1, cols // tn),
        in_specs=[
            half, half,
            pl.BlockSpec((bk, tn), lambda e, j: ((e + blocks - (nb - 1)) % blocks, j)),
            pl.BlockSpec((bk, tn), lambda e, j: ((e + blocks - nb) % blocks, j)),
        ],
        out_specs=pl.BlockSpec((1, n, tn), lambda e, j: (e, 0, j)),
        out_shape=jax.ShapeDtypeStruct((2 * nb - 1, n, cols), F32),
        compiler_params=_params("parallel", "parallel"),
    )(fl, fr, filt, filt)


def _block_conv_pass(fb_ref, gb_ref, u_ref, gate_ref, skip, h_ref, o_ref, u_scr, y_scr, rc):
    n, bk = fb_ref.shape
    nb = u_ref.shape[0] // bk
    for jb in range(nb):
        u_scr[jb] = _dot(fb_ref[...], u_ref[jb * bk:(jb + 1) * bk, :])
    row0 = lax.broadcasted_iota(jnp.int32, (rc, 1), 0) == 0
    for ib in range(nb):
        for r in range(0, bk, rc):
            re = slice(r, r + rc)
            im = slice(bk + r, bk + r + rc)
            acc_re = acc_im = None
            for jb in range(nb):
                dlt = ib - jb + nb - 1
                ure, uim = u_scr[jb, re, :], u_scr[jb, im, :]
                hre, him = h_ref[dlt, re, :], h_ref[dlt, im, :]
                if r == 0:
                    t_re = ure * hre - uim * jnp.where(row0, 0.0, him)
                    t_im = jnp.where(row0, 0.0, ure) * him + uim * jnp.where(row0, him, hre)
                else:
                    t_re = ure * hre - uim * him
                    t_im = ure * him + uim * hre
                acc_re = t_re if acc_re is None else acc_re + t_re
                acc_im = t_im if acc_im is None else acc_im + t_im
            y_scr[ib, re, :] = acc_re.astype(y_scr.dtype)
            y_scr[ib, im, :] = acc_im.astype(y_scr.dtype)
        y = _dot(gb_ref[...], y_scr[ib])
        rows = slice(ib * bk, (ib + 1) * bk)
        u = u_ref[rows, :].astype(F32)
        o_ref[rows, :] = (gate_ref[rows, :].astype(F32) * (y + u * skip)).astype(o_ref.dtype)


def _hyena_conv_kernel(fb_ref, gb_ref, v_ref, x1_ref, x2_ref, skip_ref, h1_ref, h2_ref, o_ref,
                       u_scr, y_scr, z_scr, *, rc):
    _block_conv_pass(fb_ref, gb_ref, v_ref, x1_ref, skip_ref[0:1, :], h1_ref, z_scr, u_scr, y_scr, rc)
    _block_conv_pass(fb_ref, gb_ref, z_scr, x2_ref, skip_ref[1:2, :], h2_ref, o_ref, u_scr, y_scr, rc)


def _hyena_conv(fb, gb, vxx, skip, spec, batch, seq, d, *, tn=256, rc=32):
    n, bk = fb.shape
    nb = seq // bk
    cb = d // tn
    part = lambda c: pl.BlockSpec((seq, tn), lambda j, b: (b, c * cb + j))
    spectra = lambda order: pl.BlockSpec((2 * nb - 1, n, tn), lambda j, b: (0, 0, order * cb + j))
    return pl.pallas_call(
        functools.partial(_hyena_conv_kernel, rc=rc),
        grid=(cb, batch),
        in_specs=[
            pl.BlockSpec((n, bk), lambda j, b: (0, 0)),
            pl.BlockSpec((bk, n), lambda j, b: (0, 0)),
            part(0), part(1), part(2),
            pl.BlockSpec((2, tn), lambda j, b: (0, j)),
            spectra(0), spectra(1),
        ],
        out_specs=pl.BlockSpec((seq, tn), lambda j, b: (b, j)),
        out_shape=jax.ShapeDtypeStruct((batch * seq, d), BF16),
        scratch_shapes=[pltpu.VMEM((nb, n, tn), F32), pltpu.VMEM((nb, n, tn), BF16),
                        pltpu.VMEM((seq, tn), BF16)],
        compiler_params=_params("parallel", "arbitrary"),
    )(fb, gb, vxx, vxx, vxx, skip, spec, spec)


def kernel(x, norm_mix_g, norm_ffn_g, a_w_in, a_conv_w, a_w_out, b_w_qkv, b_q_norm_g, b_k_norm_g,
           b_rpb, b_w_out, c_w_in, c_short_w, c_short_b, c_f_w1, c_f_b1, c_f_w2, c_f_b2, c_f_w3,
           c_f_freq, c_f_skip, c_w_out, f_w13, f_w2):
    batch, seq, d = x.shape
    depth = norm_mix_g.shape[0]
    heads = d // NA_HEAD_DIM
    h = x.reshape(batch * seq, d)
    bf = lambda w: w.astype(BF16)
    a_w_in, a_w_out, f_w13, f_w2 = bf(a_w_in), bf(a_w_out), bf(f_w13), bf(f_w2)
    ia = ib = ic = 0
    for i in range(depth):
        g_mix = norm_mix_g[i][None, :]
        out_proj = ()
        kind = i % 3
        if kind == 0:
            h = _sconv_mixer(h, g_mix, a_w_in, a_conv_w[ia], a_w_out, ia, seq)
            ia += 1
        elif kind == 1:
            head_gain = jnp.concatenate([
                jnp.tile(b_q_norm_g[ib], heads) * (NA_HEAD_DIM ** -0.5),
                jnp.tile(b_k_norm_g[ib], heads)])[None, :]
            qkv = _qkv(h, g_mix, bf(b_w_qkv[ib]), head_gain)
            h = _natten(qkv.reshape(batch, seq, 3 * d), _bias_table(b_rpb[ib]), bf(b_w_out[ib]),
                        h.reshape(batch, seq, d)).reshape(batch * seq, d)
            ib += 1
        else:
            fl, fr, gb = _block_dft_matrices(seq // HYENA_CONV_BLOCKS)
            filt = _hyena_filters_time(seq, d, c_f_w1[ic], c_f_b1[ic], c_f_w2[ic], c_f_b2[ic],
                                       c_f_w3[ic], c_f_freq[ic])
            spec = _filter_spectra(fl, fr, filt)
            vxx = _proj_conv(h, g_mix, bf(c_w_in[ic]), c_short_w[ic], c_short_b[ic][None, :], seq)
            z = _hyena_conv(fl, gb, vxx, c_f_skip[ic], spec, batch, seq, d)
            out_proj = (z, bf(c_w_out[ic]))
            ic += 1
        h = _ffn(h, norm_ffn_g[i][None, :], f_w13, f_w2, i, *out_proj)
    return h.reshape(batch, seq, d)
```

```python
import functools
import math

import jax
import jax.numpy as jnp
from jax import lax
from jax.experimental import pallas as pl
from jax.experimental.pallas import tpu as pltpu

F32 = jnp.float32
BF16 = jnp.bfloat16
HIGHEST = lax.Precision.HIGHEST

RMS_EPS = 1e-6
GRID_W = 64
NA_HEAD_DIM = 64
NA_WIN_ROWS = 8
NA_WIN_COLS = 16
NA_RPB_ROWS = 2 * NA_WIN_ROWS - 1
NA_RPB_COLS = 2 * NA_WIN_COLS - 1
NA_VALUES_LAG = 3
NA_ROW_UNROLL = 4
HYENA_BANDS = 16
HYENA_DECAY_TARGET = 1e-2
HYENA_FAST_DECAY = 0.3
HYENA_SLOW_DECAY = 1.5
HYENA_CONV_BLOCKS = 4
MASK_VALUE = -1e30

V7X_LANES = 128
V7X_SUBLANES = 8
V7X_MXU_DIM = 256
VMEM_LIMIT_BYTES = 56 * 1024 * 1024


def _params(*semantics):
    return pltpu.CompilerParams(dimension_semantics=semantics,
                                vmem_limit_bytes=VMEM_LIMIT_BYTES)


def _rmsnorm_bf16(x, g):
    ms = jnp.mean(x * x, axis=-1, keepdims=True)
    return (x * lax.rsqrt(ms + RMS_EPS) * g).astype(BF16)


def _dot(a, b):
    return jnp.dot(a, b, preferred_element_type=F32)


def _qkv_kernel(x_ref, g_ref, w_ref, hg_ref, o_ref):
    xn = _rmsnorm_bf16(x_ref[...], g_ref[...])
    nc = V7X_MXU_DIM
    qk_cols = hg_ref.shape[1]
    r = lax.broadcasted_iota(jnp.int32, (nc, nc), 0) // NA_HEAD_DIM
    c = lax.broadcasted_iota(jnp.int32, (nc, nc), 1) // NA_HEAD_DIM
    head_ones = jnp.where(r == c, 1.0, 0.0).astype(BF16)
    chunks = list(range(0, w_ref.shape[1], nc))

    def head_sumsq(n, acc):
        return _dot((acc * acc).astype(BF16), head_ones) if n < qk_cols else None

    def finish(n, acc, ssq):
        if ssq is not None:
            acc = acc * lax.rsqrt(ssq * (1.0 / NA_HEAD_DIM) + RMS_EPS) * hg_ref[:, n:n + nc]
        o_ref[:, n:n + nc] = acc.astype(o_ref.dtype)

    acc_q, ssq_q = {}, {}
    for t in range(len(chunks) + 2):
        if t < len(chunks):
            acc_q[t] = _dot(xn, w_ref[:, chunks[t]:chunks[t] + nc])
        if 1 <= t <= len(chunks):
            ssq_q[t - 1] = head_sumsq(chunks[t - 1], acc_q[t - 1])
        if t >= 2:
            finish(chunks[t - 2], acc_q.pop(t - 2), ssq_q.pop(t - 2))


def _qkv(x, g, w, head_gain, *, tm=512):
    t, d = x.shape
    n = w.shape[1]
    return pl.pallas_call(
        _qkv_kernel,
        grid=(t // tm,),
        in_specs=[
            pl.BlockSpec((tm, d), lambda i: (i, 0)),
            pl.BlockSpec((1, d), lambda i: (0, 0)),
            pl.BlockSpec((d, n), lambda i: (0, 0)),
            pl.BlockSpec(head_gain.shape, lambda i: (0, 0)),
        ],
        out_specs=pl.BlockSpec((tm, n), lambda i: (i, 0)),
        out_shape=jax.ShapeDtypeStruct((t, n), BF16),
        compiler_params=_params("parallel"),
    )(x, g, w, head_gain)


def _ffn_kernel(x_ref, g_ref, w13_ref, w2_ref, *rest, hc):
    o_ref, acc_ref = rest[-2:]
    x = x_ref[...]
    if len(rest) == 4:
        z_ref, wo_ref = rest[:2]
        x = x + _dot(z_ref[...], wo_ref[...])
    xn = _rmsnorm_bf16(x, g_ref[...])
    hidden = w2_ref.shape[0]
    for c in range(0, hidden, hc):
        gate = _dot(xn, w13_ref[:, c:c + hc])
        up = _dot(xn, w13_ref[:, hidden + c:hidden + c + hc])
        act = (gate * jax.nn.sigmoid(gate) * up).astype(BF16)
        part = _dot(act, w2_ref[c:c + hc, :])
        if c == 0:
            acc_ref[...] = part
        else:
            acc_ref[...] += part
    o_ref[...] = x + acc_ref[...]


def _ffn(x, g, w13, w2, layer, z=None, w_out=None, *, tm=512, hc=256):
    t, d = x.shape
    hidden = w2.shape[1]
    once = pl.Buffered(1)
    in_specs = [
        pl.BlockSpec((tm, d), lambda i: (i, 0)),
        pl.BlockSpec((1, d), lambda i: (0, 0)),
        pl.BlockSpec((None, d, 2 * hidden), lambda i: (layer, 0, 0), pipeline_mode=once),
        pl.BlockSpec((None, hidden, d), lambda i: (layer, 0, 0), pipeline_mode=once),
    ]
    args = [x, g, w13, w2]
    if z is not None:
        in_specs += [pl.BlockSpec((tm, d), lambda i: (i, 0)),
                     pl.BlockSpec((d, d), lambda i: (0, 0), pipeline_mode=once)]
        args += [z, w_out]
    return pl.pallas_call(
        functools.partial(_ffn_kernel, hc=hc),
        grid=(t // tm,),
        in_specs=in_specs,
        out_specs=pl.BlockSpec((tm, d), lambda i: (i, 0)),
        out_shape=jax.ShapeDtypeStruct((t, d), F32),
        scratch_shapes=[pltpu.VMEM((tm, d), F32)],
        compiler_params=_params("parallel"),
    )(*args)


def _shift_rows(v, prev_row, next_row):
    tm, n = v.shape
    sub = V7X_SUBLANES
    groups = tm // sub
    v3 = v.reshape(groups, sub, n)
    row = lax.broadcasted_iota(jnp.int32, (1, sub, 1), 1)
    down = pltpu.roll(v3, 1, axis=1)
    up = pltpu.roll(v3, sub - 1, axis=1)
    halo = lambda r: jnp.broadcast_to(r[None], (1, sub, n))
    down_nb = jnp.concatenate([halo(prev_row), down[:groups - 1]], axis=0)
    up_nb = jnp.concatenate([up[1:], halo(next_row)], axis=0)
    v_prev = jnp.where(row == 0, down_nb, down)
    v_next = jnp.where(row == sub - 1, up_nb, up)
    return v_prev.reshape(tm, n), v_next.reshape(tm, n)


def _conv3(v, prev_row, next_row, w):
    v_prev, v_next = _shift_rows(v, prev_row, next_row)
    return w[0:1] * v_prev + w[1:2] * v + w[2:3] * v_next


def _finished_tile_edges(i, tiles_per_seq):
    pos = (i - 1) % tiles_per_seq
    return pos == 0, pos == tiles_per_seq - 1


def _delayed_specs(tm, n_tiles):
    current = lambda i: (jnp.minimum(i, n_tiles - 1), 0)
    finished = lambda i: (jnp.maximum(i - 1, 0), 0)
    const = lambda i: (0, 0)
    return current, finished, const


def _sconv_mixer_kernel(x_ref, xf_ref, g_ref, win_ref, cw_ref, wo_ref, o_ref, b_scr, v_scr, edge_scr,
                        *, tiles_per_seq):
    i = pl.program_id(0)
    tm, d = x_ref.shape

    @pl.when(i == 0)
    def _init():
        b_scr[...] = jnp.zeros_like(b_scr)
        v_scr[...] = jnp.zeros_like(v_scr)
        edge_scr[...] = jnp.zeros_like(edge_scr)

    xn = _rmsnorm_bf16(x_ref[...], g_ref[...])
    b_new = _dot(xn, win_ref[:, 0:d]).astype(BF16)
    v_new = _dot(xn, win_ref[:, d:2 * d]) * _dot(xn, win_ref[:, 2 * d:3 * d])

    first, last = _finished_tile_edges(i, tiles_per_seq)
    v = v_scr[...]
    prev_row = jnp.where(first, 0.0, edge_scr[...])
    next_row = jnp.where(last, 0.0, v_new[0:1, :])
    conv = _conv3(v, prev_row, next_row, cw_ref[...])
    y = (b_scr[...].astype(F32) * conv).astype(BF16)
    o_ref[...] = xf_ref[...] + _dot(y, wo_ref[...])

    edge_scr[...] = v[tm - 1:tm, :]
    b_scr[...] = b_new
    v_scr[...] = v_new


def _sconv_mixer(x, g, w_in, conv_w, w_out, layer, seq, *, tm=512):
    t, d = x.shape
    n_tiles = t // tm
    current, finished, const = _delayed_specs(tm, n_tiles)
    stacked = lambda i: (layer, 0, 0)
    return pl.pallas_call(
        functools.partial(_sconv_mixer_kernel, tiles_per_seq=seq // tm),
        grid=(n_tiles + 1,),
        in_specs=[
            pl.BlockSpec((tm, d), current),
            pl.BlockSpec((tm, d), finished),
            pl.BlockSpec((1, d), const),
            pl.BlockSpec((None, d, 3 * d), stacked),
            pl.BlockSpec((3, d), const),
            pl.BlockSpec((None, d, d), stacked),
        ],
        out_specs=pl.BlockSpec((tm, d), finished),
        out_shape=jax.ShapeDtypeStruct((t, d), F32),
        scratch_shapes=[pltpu.VMEM((tm, d), BF16),
                        pltpu.VMEM((tm, d), F32),
                        pltpu.VMEM((1, d), F32)],
        compiler_params=_params("arbitrary"),
    )(x, x, g, w_in, conv_w, w_out)


def _proj_conv_kernel(x_ref, g_ref, w_ref, cw_ref, cb_ref, o_ref, p_scr, edge_scr, *, tiles_per_seq, nc):
    i = pl.program_id(0)
    tm = x_ref.shape[0]

    @pl.when(i == 0)
    def _init():
        p_scr[...] = jnp.zeros_like(p_scr)
        edge_scr[...] = jnp.zeros_like(edge_scr)

    xn = _rmsnorm_bf16(x_ref[...], g_ref[...])
    first, last = _finished_tile_edges(i, tiles_per_seq)
    for c in range(0, w_ref.shape[1], nc):
        cols = slice(c, c + nc)
        p_new = _dot(xn, w_ref[:, cols])
        p = p_scr[:, cols]
        prev_row = jnp.where(first, 0.0, edge_scr[:, cols])
        next_row = jnp.where(last, 0.0, p_new[0:1, :])
        conv = _conv3(p, prev_row, next_row, cw_ref[:, cols])
        o_ref[:, cols] = (conv + cb_ref[:, cols]).astype(o_ref.dtype)
        edge_scr[:, cols] = p[tm - 1:tm, :]
        p_scr[:, cols] = p_new


def _proj_conv(x, g, w, conv_w, conv_b, seq, *, tm=512, nc=512):
    t, d = x.shape
    n = w.shape[1]
    n_tiles = t // tm
    current, finished, const = _delayed_specs(tm, n_tiles)
    return pl.pallas_call(
        functools.partial(_proj_conv_kernel, tiles_per_seq=seq // tm, nc=nc),
        grid=(n_tiles + 1,),
        in_specs=[
            pl.BlockSpec((tm, d), current),
            pl.BlockSpec((1, d), const),
            pl.BlockSpec((d, n), const),
            pl.BlockSpec((3, n), const),
            pl.BlockSpec((1, n), const),
        ],
        out_specs=pl.BlockSpec((tm, n), finished),
        out_shape=jax.ShapeDtypeStruct((t, n), BF16),
        scratch_shapes=[pltpu.VMEM((tm, n), F32),
                        pltpu.VMEM((1, n), F32)],
        compiler_params=_params("arbitrary"),
    )(x, g, w, conv_w, conv_b)


def _bias_table_kernel(r_ref, o_ref):
    kdim = r_ref.shape[1]
    n = o_ref.shape[1]

    def decode(shape):
        c = lax.broadcasted_iota(jnp.int32, shape, 1)
        qc = c // (2 * GRID_W)
        lane = c % (2 * GRID_W)
        second = lane >= GRID_W
        kc = jnp.where(second, lane - GRID_W, lane)
        start = jnp.clip(qc - NA_WIN_COLS // 2, 0, GRID_W - NA_WIN_COLS)
        valid = (kc >= start) & (kc < start + NA_WIN_COLS)
        idx = kc - qc + (NA_WIN_COLS - 1) + jnp.where(second, NA_RPB_COLS, 0)
        return valid, idx

    valid, idx = decode((kdim, n))
    i = lax.broadcasted_iota(jnp.int32, (kdim, n), 0)
    onehot = jnp.where(valid & (i == idx), 1.0, 0.0)
    tab = jnp.dot(r_ref[...], onehot, preferred_element_type=F32, precision=HIGHEST)
    valid_row, _ = decode((1, n))
    o_ref[...] = jnp.where(valid_row, tab, MASK_VALUE)


def _bias_table(rpb):
    heads = rpb.shape[0]
    pairs = NA_RPB_ROWS - 1
    kdim = 64
    pad = jnp.zeros((pairs, heads, kdim - 2 * NA_RPB_COLS), F32)
    rows = jnp.swapaxes(rpb, 0, 1)
    r = jnp.concatenate([rows[:-1], rows[1:], pad], axis=-1).reshape(pairs * heads, kdim)
    n = GRID_W * 2 * GRID_W
    tab = pl.pallas_call(
        _bias_table_kernel,
        out_shape=jax.ShapeDtypeStruct((pairs * heads, n), F32),
        compiler_params=pltpu.CompilerParams(vmem_limit_bytes=VMEM_LIMIT_BYTES),
    )(r)
    return tab.reshape(pairs, heads * GRID_W, 2 * GRID_W)


def _natten_kernel(q_ref, k_ref, v_ref, tab_ref, wo_ref, x_ref, o_ref, att_scr, *, rows_per_step, n_rows):
    pairs = q_ref.shape[2] // V7X_LANES
    lane = lax.broadcasted_iota(jnp.int32, (GRID_W, V7X_LANES), 1)
    low = lane < NA_HEAD_DIM
    win = NA_WIN_ROWS * GRID_W

    def row_body(rr, carry):
        r = pl.program_id(1) * rows_per_step + rr
        rs = jnp.clip(r - NA_WIN_ROWS // 2, 0, n_rows - NA_WIN_ROWS)
        d0 = rs - r + NA_WIN_ROWS - 1
        q_rows = pl.ds(pl.multiple_of(rr * GRID_W, GRID_W), GRID_W)
        k_rows = pl.ds(pl.multiple_of(rs * GRID_W, GRID_W), win)

        def scores(hp):
            cols = slice(hp * V7X_LANES, (hp + 1) * V7X_LANES)
            q2 = q_ref[0, q_rows, cols]
            zero = jnp.zeros_like(q2)
            qs = jnp.concatenate([jnp.where(low, q2, zero), jnp.where(low, zero, q2)], axis=0)
            return lax.dot_general(qs, k_ref[0, k_rows, cols], (((1,), (1,)), ((), ())),
                                   preferred_element_type=F32)

        def softmax(hp, s):
            rows = slice(hp * 2 * GRID_W, (hp + 1) * 2 * GRID_W)
            bias = jnp.concatenate(
                [tab_ref[pl.ds(d0 + 2 * p, 1), rows, :][0] for p in range(NA_WIN_ROWS // 2)], axis=1)
            s = s + bias
            m = jnp.max(s, axis=-1, keepdims=True)
            e = jnp.exp(s - m)
            return e.astype(BF16), 1.0 / jnp.sum(e, axis=-1, keepdims=True)

        def values(hp, p, linv):
            cols = slice(hp * V7X_LANES, (hp + 1) * V7X_LANES)
            o = _dot(p, v_ref[0, k_rows, cols]) * linv
            att_scr[q_rows, cols] = jnp.where(low, o[:GRID_W], o[GRID_W:]).astype(att_scr.dtype)

        s_q, p_q = {}, {}
        for t in range(pairs + 1 + NA_VALUES_LAG):
            if t < pairs:
                s_q[t] = scores(t)
            if 1 <= t <= pairs:
                p_q[t - 1] = softmax(t - 1, s_q.pop(t - 1))
            if t >= 1 + NA_VALUES_LAG:
                hp = t - 1 - NA_VALUES_LAG
                values(hp, *p_q.pop(hp))
        return carry

    lax.fori_loop(0, rows_per_step, row_body, 0, unroll=NA_ROW_UNROLL)
    o_ref[0] = x_ref[0] + _dot(att_scr[...], wo_ref[...])


def _natten(qkv, tab, w_out, x, *, rows_per_step=16):
    b, s, d3 = qkv.shape
    d = d3 // 3
    n_rows = s // GRID_W
    qs = rows_per_step * GRID_W
    once = pl.Buffered(1)
    return pl.pallas_call(
        functools.partial(_natten_kernel, rows_per_step=rows_per_step, n_rows=n_rows),
        grid=(b, n_rows // rows_per_step),
        in_specs=[
            pl.BlockSpec((1, qs, d), lambda i, j: (i, j, 0)),
            pl.BlockSpec((1, s, d), lambda i, j: (i, 0, 1)),
            pl.BlockSpec((1, s, d), lambda i, j: (i, 0, 2)),
            pl.BlockSpec(tab.shape, lambda i, j: (0, 0, 0), pipeline_mode=once),
            pl.BlockSpec((d, d), lambda i, j: (0, 0), pipeline_mode=once),
            pl.BlockSpec((1, qs, d), lambda i, j: (i, j, 0)),
        ],
        out_specs=pl.BlockSpec((1, qs, d), lambda i, j: (i, j, 0)),
        out_shape=jax.ShapeDtypeStruct((b, s, d), F32),
        scratch_shapes=[pltpu.VMEM((qs, d), BF16)],
        compiler_params=_params("parallel", "arbitrary"),
    )(qkv, qkv, qkv, tab, w_out, x)


def _block_dft_matrices(bk):
    n = 2 * bk
    row = jnp.arange(n, dtype=jnp.int32)
    keff = jnp.where(row <= bk, row, row - bk)
    quarter = jnp.where(row > bk, n // 4, 0)
    t = jnp.arange(n, dtype=jnp.int32)
    turns = (keff[:, None] * t[None, :] + quarter[:, None]) % n
    full = jnp.cos(turns.astype(F32) * (2.0 * math.pi / n))
    weight = jnp.where((row == 0) | (row == bk), 1.0 / n, 2.0 / n)
    inv = (full[:, :bk] * weight[:, None]).T
    return full[:, :bk].astype(BF16), full[:, bk:].astype(BF16), inv.astype(BF16)


def _filter_kernel(bands_ref, w1t_ref, w1c_ref, w1s_ref, b1_ref, w2_ref, b2_ref, freq_ref,
                   w3f_ref, w3r_ref, delta_ref, h_ref, hid_scr):
    seq = h_ref.shape[0] // 2
    j = lax.broadcasted_iota(jnp.int32, (seq, 1), 0)
    pos_f = j.astype(F32)
    pos_r = jnp.where(j == 0, 0, seq - j).astype(F32)

    def mlp(reverse):
        lane = lax.broadcasted_iota(jnp.int32, (1, seq), 1)
        pos = (jnp.where(lane == 0, 0, seq - lane) if reverse else lane).astype(F32)
        t = pos / (seq - 1.0)
        ang = (2.0 * math.pi) * pos / seq * bands_ref[...]
        pre = (t * w1t_ref[...]
               + jnp.dot(w1c_ref[...], jnp.cos(ang), preferred_element_type=F32, precision=HIGHEST)
               + jnp.dot(w1s_ref[...], -jnp.sin(ang), preferred_element_type=F32, precision=HIGHEST)
               + b1_ref[...])
        hid = jnp.sin(freq_ref[...] * pre)
        hid = jnp.sin(freq_ref[...] * (
            jnp.dot(w2_ref[...], hid, preferred_element_type=F32, precision=HIGHEST) + b2_ref[...]))
        return hid.T

    @pl.when((pl.program_id(0) == 0) & (pl.program_id(1) == 0))
    def _hidden():
        hid_scr[0] = mlp(False)
        hid_scr[1] = mlp(True)

    def filt(hid, w3_ref, pos):
        f = jnp.dot(hid, w3_ref[...], preferred_element_type=F32, precision=HIGHEST)
        return f * jnp.exp(-(pos / (seq - 1.0)) * delta_ref[...])

    fwd = filt(hid_scr[0], w3f_ref, pos_f)
    rev = filt(hid_scr[1], w3r_ref, pos_r)
    top = fwd + jnp.where(j == 0, rev, 0.0)
    bot = jnp.where(j == 0, 0.0, rev)
    l1 = jnp.sum(jnp.abs(top), axis=0, keepdims=True) + jnp.sum(jnp.abs(bot), axis=0, keepdims=True)
    inv = 1.0 / l1
    h_ref[0:seq, :] = (top * inv).astype(h_ref.dtype)
    h_ref[seq:2 * seq, :] = (bot * inv).astype(h_ref.dtype)


def _hyena_filters_time(seq, d, w1, b1, w2, b2, w3, freq, *, tn=512):
    nb = HYENA_BANDS
    hid = w2.shape[0]
    bands = jnp.linspace(1e-4, nb - 1, nb, dtype=F32)[:, None]
    lt = math.log(HYENA_DECAY_TARGET)
    deltas = jnp.abs(jnp.linspace(lt / HYENA_SLOW_DECAY, lt / HYENA_FAST_DECAY, d, dtype=F32))[None, :]
    small = lambda shape: pl.BlockSpec(shape, lambda o, c: (0, 0))
    cpo = d // tn
    return pl.pallas_call(
        _filter_kernel,
        grid=(2, cpo),
        in_specs=[
            small((nb, 1)), small((hid, 1)), small((hid, nb)), small((hid, nb)), small((hid, 1)),
            small((hid, hid)), small((hid, 1)), small((hid, 1)),
            pl.BlockSpec((hid, tn), lambda o, c: (0, 2 * o * cpo + c)),
            pl.BlockSpec((hid, tn), lambda o, c: (0, (2 * o + 1) * cpo + c)),
            pl.BlockSpec((1, tn), lambda o, c: (0, c)),
        ],
        out_specs=pl.BlockSpec((2 * seq, tn), lambda o, c: (0, o * cpo + c)),
        out_shape=jax.ShapeDtypeStruct((2 * seq, 2 * d), BF16),
        scratch_shapes=[pltpu.VMEM((2, seq, hid), F32)],
        compiler_params=_params("arbitrary", "arbitrary"),
    )(bands, w1[0:1].T, w1[1:1 + nb].T, w1[1 + nb:1 + 2 * nb].T, b1[:, None], w2.T, b2[:, None],
      freq[:, None], w3, w3, deltas)


def _filter_spectra_kernel(fl_ref, fr_ref, top_ref, bot_ref, o_ref):
    o_ref[0] = _dot(fl_ref[...], top_ref[...]) + _dot(fr_ref[...], bot_ref[...])


def _filter_spectra(fl, fr, filt, *, tn=1024):
    n, bk = fl.shape
    blocks = filt.shape[0] // bk
    nb = blocks // 2
    cols = filt.shape[1]
    half = pl.BlockSpec((n, bk), lambda e, j: (0, 0))
    return pl.pallas_call(
        _filter_spectra_kernel,
        grid=(2 * nb - 1, cols // tn),
        in_specs=[
            half, half,
            pl.BlockSpec((bk, tn), lambda e, j: ((e + blocks - (nb - 1)) % blocks, j)),
            pl.BlockSpec((bk, tn), lambda e, j: ((e + blocks - nb) % blocks, j)),
        ],
        out_specs=pl.BlockSpec((1, n, tn), lambda e, j: (e, 0, j)),
        out_shape=jax.ShapeDtypeStruct((2 * nb - 1, n, cols), F32),
        compiler_params=_params("parallel", "parallel"),
    )(fl, fr, filt, filt)


def _block_conv_pass(fb_ref, gb_ref, u_ref, gate_ref, skip, h_ref, o_ref, u_scr, y_scr, rc):
    n, bk = fb_ref.shape
    nb = u_ref.shape[0] // bk
    for jb in range(nb):
        u_scr[jb] = _dot(fb_ref[...], u_ref[jb * bk:(jb + 1) * bk, :])
    row0 = lax.broadcasted_iota(jnp.int32, (rc, 1), 0) == 0
    for ib in range(nb):
        for r in range(0, bk, rc):
            re = slice(r, r + rc)
            im = slice(bk + r, bk + r + rc)
            acc_re = acc_im = None
            for jb in range(nb):
                dlt = ib - jb + nb - 1
                ure, uim = u_scr[jb, re, :], u_scr[jb, im, :]
                hre, him = h_ref[dlt, re, :], h_ref[dlt, im, :]
                if r == 0:
                    t_re = ure * hre - uim * jnp.where(row0, 0.0, him)
                    t_im = jnp.where(row0, 0.0, ure) * him + uim * jnp.where(row0, him, hre)
                else:
                    t_re = ure * hre - uim * him
                    t_im = ure * him + uim * hre
                acc_re = t_re if acc_re is None else acc_re + t_re
                acc_im = t_im if acc_im is None else acc_im + t_im
            y_scr[ib, re, :] = acc_re.astype(y_scr.dtype)
            y_scr[ib, im, :] = acc_im.astype(y_scr.dtype)
        y = _dot(gb_ref[...], y_scr[ib])
        rows = slice(ib * bk, (ib + 1) * bk)
        u = u_ref[rows, :].astype(F32)
        o_ref[rows, :] = (gate_ref[rows, :].astype(F32) * (y + u * skip)).astype(o_ref.dtype)


def _hyena_conv_kernel(fb_ref, gb_ref, v_ref, x1_ref, x2_ref, skip_ref, h1_ref, h2_ref, o_ref,
                       u_scr, y_scr, z_scr, *, rc):
    _block_conv_pass(fb_ref, gb_ref, v_ref, x1_ref, skip_ref[0:1, :], h1_ref, z_scr, u_scr, y_scr, rc)
    _block_conv_pass(fb_ref, gb_ref, z_scr, x2_ref, skip_ref[1:2, :], h2_ref, o_ref, u_scr, y_scr, rc)


def _hyena_conv(fb, gb, vxx, skip, spec, batch, seq, d, *, tn=256, rc=32):
    n, bk = fb.shape
    nb = seq // bk
    cb = d // tn
    part = lambda c: pl.BlockSpec((seq, tn), lambda j, b: (b, c * cb + j))
    spectra = lambda order: pl.BlockSpec((2 * nb - 1, n, tn), lambda j, b: (0, 0, order * cb + j))
    return pl.pallas_call(
        functools.partial(_hyena_conv_kernel, rc=rc),
        grid=(cb, batch),
        in_specs=[
            pl.BlockSpec((n, bk), lambda j, b: (0, 0)),
            pl.BlockSpec((bk, n), lambda j, b: (0, 0)),
            part(0), part(1), part(2),
            pl.BlockSpec((2, tn), lambda j, b: (0, j)),
            spectra(0), spectra(1),
        ],
        out_specs=pl.BlockSpec((seq, tn), lambda j, b: (b, j)),
        out_shape=jax.ShapeDtypeStruct((batch * seq, d), BF16),
        scratch_shapes=[pltpu.VMEM((nb, n, tn), F32), pltpu.VMEM((nb, n, tn), BF16),
                        pltpu.VMEM((seq, tn), BF16)],
        compiler_params=_params("parallel", "arbitrary"),
    )(fb, gb, vxx, vxx, vxx, skip, spec, spec)


def kernel(x, norm_mix_g, norm_ffn_g, a_w_in, a_conv_w, a_w_out, b_w_qkv, b_q_norm_g, b_k_norm_g,
           b_rpb, b_w_out, c_w_in, c_short_w, c_short_b, c_f_w1, c_f_b1, c_f_w2, c_f_b2, c_f_w3,
           c_f_freq, c_f_skip, c_w_out, f_w13, f_w2):
    batch, seq, d = x.shape
    depth = norm_mix_g.shape[0]
    heads = d // NA_HEAD_DIM
    h = x.reshape(batch * seq, d)
    bf = lambda w: w.astype(BF16)
    a_w_in, a_w_out, f_w13, f_w2 = bf(a_w_in), bf(a_w_out), bf(f_w13), bf(f_w2)
    ia = ib = ic = 0
    for i in range(depth):
        g_mix = norm_mix_g[i][None, :]
        out_proj = ()
        kind = i % 3
        if kind == 0:
            h = _sconv_mixer(h, g_mix, a_w_in, a_conv_w[ia], a_w_out, ia, seq)
            ia += 1
        elif kind == 1:
            head_gain = jnp.concatenate([
                jnp.tile(b_q_norm_g[ib], heads) * (NA_HEAD_DIM ** -0.5),
                jnp.tile(b_k_norm_g[ib], heads)])[None, :]
            qkv = _qkv(h, g_mix, bf(b_w_qkv[ib]), head_gain)
            h = _natten(qkv.reshape(batch, seq, 3 * d), _bias_table(b_rpb[ib]), bf(b_w_out[ib]),
                        h.reshape(batch, seq, d)).reshape(batch * seq, d)
            ib += 1
        else:
            fl, fr, gb = _block_dft_matrices(seq // HYENA_CONV_BLOCKS)
            filt = _hyena_filters_time(seq, d, c_f_w1[ic], c_f_b1[ic], c_f_w2[ic], c_f_b2[ic],
                                       c_f_w3[ic], c_f_freq[ic])
            spec = _filter_spectra(fl, fr, filt)
            vxx = _proj_conv(h, g_mix, bf(c_w_in[ic]), c_short_w[ic], c_short_b[ic][None, :], seq)
            z = _hyena_conv(fl, gb, vxx, c_f_skip[ic], spec, batch, seq, d)
            out_proj = (z, bf(c_w_out[ic]))
            ic += 1
        h = _ffn(h, norm_ffn_g[i][None, :], f_w13, f_w2, i, *out_proj)
    return h.reshape(batch, seq, d)
```

```python
import functools
import math

import jax
import jax.numpy as jnp
from jax import lax
from jax.experimental import pallas as pl
from jax.experimental.pallas import tpu as pltpu

F32 = jnp.float32
BF16 = jnp.bfloat16
HIGHEST = lax.Precision.HIGHEST

RMS_EPS = 1e-6
GRID_W = 64
NA_HEAD_DIM = 64
NA_WIN_ROWS = 8
NA_WIN_COLS = 16
NA_RPB_ROWS = 2 * NA_WIN_ROWS - 1
NA_RPB_COLS = 2 * NA_WIN_COLS - 1
NA_VALUES_LAG = 3
NA_ROW_UNROLL = 4
HYENA_BANDS = 16
HYENA_DECAY_TARGET = 1e-2
HYENA_FAST_DECAY = 0.3
HYENA_SLOW_DECAY = 1.5
HYENA_CONV_BLOCKS = 4
MASK_VALUE = -1e30

V7X_LANES = 128
V7X_SUBLANES = 8
V7X_MXU_DIM = 256
VMEM_LIMIT_BYTES = 56 * 1024 * 1024


def _params(*semantics):
    return pltpu.CompilerParams(dimension_semantics=semantics,
                                vmem_limit_bytes=VMEM_LIMIT_BYTES)


def _rmsnorm_bf16(x, g):
    ms = jnp.mean(x * x, axis=-1, keepdims=True)
    return (x * lax.rsqrt(ms + RMS_EPS) * g).astype(BF16)


def _dot(a, b):
    return jnp.dot(a, b, preferred_element_type=F32)


def _qkv_kernel(x_ref, g_ref, w_ref, hg_ref, o_ref):
    xn = _rmsnorm_bf16(x_ref[...], g_ref[...])
    nc = V7X_MXU_DIM
    qk_cols = hg_ref.shape[1]
    r = lax.broadcasted_iota(jnp.int32, (nc, nc), 0) // NA_HEAD_DIM
    c = lax.broadcasted_iota(jnp.int32, (nc, nc), 1) // NA_HEAD_DIM
    head_ones = jnp.where(r == c, 1.0, 0.0).astype(BF16)
    chunks = list(range(0, w_ref.shape[1], nc))

    def head_sumsq(n, acc):
        return _dot((acc * acc).astype(BF16), head_ones) if n < qk_cols else None

    def finish(n, acc, ssq):
        if ssq is not None:
            acc = acc * lax.rsqrt(ssq * (1.0 / NA_HEAD_DIM) + RMS_EPS) * hg_ref[:, n:n + nc]
        o_ref[:, n:n + nc] = acc.astype(o_ref.dtype)

    acc_q, ssq_q = {}, {}
    for t in range(len(chunks) + 2):
        if t < len(chunks):
            acc_q[t] = _dot(xn, w_ref[:, chunks[t]:chunks[t] + nc])
        if 1 <= t <= len(chunks):
            ssq_q[t - 1] = head_sumsq(chunks[t - 1], acc_q[t - 1])
        if t >= 2:
            finish(chunks[t - 2], acc_q.pop(t - 2), ssq_q.pop(t - 2))


def _qkv(x, g, w, head_gain, *, tm=512):
    t, d = x.shape
    n = w.shape[1]
    return pl.pallas_call(
        _qkv_kernel,
        grid=(t // tm,),
        in_specs=[
            pl.BlockSpec((tm, d), lambda i: (i, 0)),
            pl.BlockSpec((1, d), lambda i: (0, 0)),
            pl.BlockSpec((d, n), lambda i: (0, 0)),
            pl.BlockSpec(head_gain.shape, lambda i: (0, 0)),
        ],
        out_specs=pl.BlockSpec((tm, n), lambda i: (i, 0)),
        out_shape=jax.ShapeDtypeStruct((t, n), BF16),
        compiler_params=_params("parallel"),
    )(x, g, w, head_gain)


def _ffn_kernel(x_ref, g_ref, w13_ref, w2_ref, *rest, hc):
    o_ref, acc_ref = rest[-2:]
    x = x_ref[...]
    if len(rest) == 4:
        z_ref, wo_ref = rest[:2]
        x = x + _dot(z_ref[...], wo_ref[...])
    xn = _rmsnorm_bf16(x, g_ref[...])
    hidden = w2_ref.shape[0]
    for c in range(0, hidden, hc):
        gate = _dot(xn, w13_ref[:, c:c + hc])
        up = _dot(xn, w13_ref[:, hidden + c:hidden + c + hc])
        act = (gate * jax.nn.sigmoid(gate) * up).astype(BF16)
        part = _dot(act, w2_ref[c:c + hc, :])
        if c == 0:
            acc_ref[...] = part
        else:
            acc_ref[...] += part
    o_ref[...] = x + acc_ref[...]


def _ffn(x, g, w13, w2, layer, z=None, w_out=None, *, tm=512, hc=256):
    t, d = x.shape
    hidden = w2.shape[1]
    once = pl.Buffered(1)
    in_specs = [
        pl.BlockSpec((tm, d), lambda i: (i, 0)),
        pl.BlockSpec((1, d), lambda i: (0, 0)),
        pl.BlockSpec((None, d, 2 * hidden), lambda i: (layer, 0, 0), pipeline_mode=once),
        pl.BlockSpec((None, hidden, d), lambda i: (layer, 0, 0), pipeline_mode=once),
    ]
    args = [x, g, w13, w2]
    if z is not None:
        in_specs += [pl.BlockSpec((tm, d), lambda i: (i, 0)),
                     pl.BlockSpec((d, d), lambda i: (0, 0), pipeline_mode=once)]
        args += [z, w_out]
    return pl.pallas_call(
        functools.partial(_ffn_kernel, hc=hc),
        grid=(t // tm,),
        in_specs=in_specs,
        out_specs=pl.BlockSpec((tm, d), lambda i: (i, 0)),
        out_shape=jax.ShapeDtypeStruct((t, d), F32),
        scratch_shapes=[pltpu.VMEM((tm, d), F32)],
        compiler_params=_params("parallel"),
    )(*args)


def _shift_rows(v, prev_row, next_row):
    tm, n = v.shape
    sub = V7X_SUBLANES
    groups = tm // sub
    v3 = v.reshape(groups, sub, n)
    row = lax.broadcasted_iota(jnp.int32, (1, sub, 1), 1)
    down = pltpu.roll(v3, 1, axis=1)
    up = pltpu.roll(v3, sub - 1, axis=1)
    halo = lambda r: jnp.broadcast_to(r[None], (1, sub, n))
    down_nb = jnp.concatenate([halo(prev_row), down[:groups - 1]], axis=0)
    up_nb = jnp.concatenate([up[1:], halo(next_row)], axis=0)
    v_prev = jnp.where(row == 0, down_nb, down)
    v_next = jnp.where(row == sub - 1, up_nb, up)
    return v_prev.reshape(tm, n), v_next.reshape(tm, n)


def _conv3(v, prev_row, next_row, w):
    v_prev, v_next = _shift_rows(v, prev_row, next_row)
    return w[0:1] * v_prev + w[1:2] * v + w[2:3] * v_next


def _finished_tile_edges(i, tiles_per_seq):
    pos = (i - 1) % tiles_per_seq
    return pos == 0, pos == tiles_per_seq - 1


def _delayed_specs(tm, n_tiles):
    current = lambda i: (jnp.minimum(i, n_tiles - 1), 0)
    finished = lambda i: (jnp.maximum(i - 1, 0), 0)
    const = lambda i: (0, 0)
    return current, finished, const


def _sconv_mixer_kernel(x_ref, xf_ref, g_ref, win_ref, cw_ref, wo_ref, o_ref, b_scr, v_scr, edge_scr,
                        *, tiles_per_seq):
    i = pl.program_id(0)
    tm, d = x_ref.shape

    @pl.when(i == 0)
    def _init():
        b_scr[...] = jnp.zeros_like(b_scr)
        v_scr[...] = jnp.zeros_like(v_scr)
        edge_scr[...] = jnp.zeros_like(edge_scr)

    xn = _rmsnorm_bf16(x_ref[...], g_ref[...])
    b_new = _dot(xn, win_ref[:, 0:d]).astype(BF16)
    v_new = _dot(xn, win_ref[:, d:2 * d]) * _dot(xn, win_ref[:, 2 * d:3 * d])

    first, last = _finished_tile_edges(i, tiles_per_seq)
    v = v_scr[...]
    prev_row = jnp.where(first, 0.0, edge_scr[...])
    next_row = jnp.where(last, 0.0, v_new[0:1, :])
    conv = _conv3(v, prev_row, next_row, cw_ref[...])
    y = (b_scr[...].astype(F32) * conv).astype(BF16)
    o_ref[...] = xf_ref[...] + _dot(y, wo_ref[...])

    edge_scr[...] = v[tm - 1:tm, :]
    b_scr[...] = b_new
    v_scr[...] = v_new


def _sconv_mixer(x, g, w_in, conv_w, w_out, layer, seq, *, tm=1024):
    t, d = x.shape
    n_tiles = t // tm
    current, finished, const = _delayed_specs(tm, n_tiles)
    stacked = lambda i: (layer, 0, 0)
    return pl.pallas_call(
        functools.partial(_sconv_mixer_kernel, tiles_per_seq=seq // tm),
        grid=(n_tiles + 1,),
        in_specs=[
            pl.BlockSpec((tm, d), current),
            pl.BlockSpec((tm, d), finished),
            pl.BlockSpec((1, d), const),
            pl.BlockSpec((None, d, 3 * d), stacked),
            pl.BlockSpec((3, d), const),
            pl.BlockSpec((None, d, d), stacked),
        ],
        out_specs=pl.BlockSpec((tm, d), finished),
        out_shape=jax.ShapeDtypeStruct((t, d), F32),
        scratch_shapes=[pltpu.VMEM((tm, d), BF16),
                        pltpu.VMEM((tm, d), F32),
                        pltpu.VMEM((1, d), F32)],
        compiler_params=_params("arbitrary"),
    )(x, x, g, w_in, conv_w, w_out)


def _proj_conv_kernel(x_ref, g_ref, w_ref, cw_ref, cb_ref, o_ref, p_scr, edge_scr, *, tiles_per_seq, nc):
    i = pl.program_id(0)
    tm = x_ref.shape[0]

    @pl.when(i == 0)
    def _init():
        p_scr[...] = jnp.zeros_like(p_scr)
        edge_scr[...] = jnp.zeros_like(edge_scr)

    xn = _rmsnorm_bf16(x_ref[...], g_ref[...])
    first, last = _finished_tile_edges(i, tiles_per_seq)
    for c in range(0, w_ref.shape[1], nc):
        cols = slice(c, c + nc)
        p_new = _dot(xn, w_ref[:, cols])
        p = p_scr[:, cols]
        prev_row = jnp.where(first, 0.0, edge_scr[:, cols])
        next_row = jnp.where(last, 0.0, p_new[0:1, :])
        conv = _conv3(p, prev_row, next_row, cw_ref[:, cols])
        o_ref[:, cols] = (conv + cb_ref[:, cols]).astype(o_ref.dtype)
        edge_scr[:, cols] = p[tm - 1:tm, :]
        p_scr[:, cols] = p_new


def _proj_conv(x, g, w, conv_w, conv_b, seq, *, tm=1024, nc=512):
    t, d = x.shape
    n = w.shape[1]
    n_tiles = t // tm
    current, finished, const = _delayed_specs(tm, n_tiles)
    return pl.pallas_call(
        functools.partial(_proj_conv_kernel, tiles_per_seq=seq // tm, nc=nc),
        grid=(n_tiles + 1,),
        in_specs=[
            pl.BlockSpec((tm, d), current),
            pl.BlockSpec((1, d), const),
            pl.BlockSpec((d, n), const),
            pl.BlockSpec((3, n), const),
            pl.BlockSpec((1, n), const),
        ],
        out_specs=pl.BlockSpec((tm, n), finished),
        out_shape=jax.ShapeDtypeStruct((t, n), BF16),
        scratch_shapes=[pltpu.VMEM((tm, n), F32),
                        pltpu.VMEM((1, n), F32)],
        compiler_params=_params("arbitrary"),
    )(x, g, w, conv_w, conv_b)


def _bias_table_kernel(r_ref, o_ref):
    kdim = r_ref.shape[1]
    n = o_ref.shape[1]

    def decode(shape):
        c = lax.broadcasted_iota(jnp.int32, shape, 1)
        qc = c // (2 * GRID_W)
        lane = c % (2 * GRID_W)
        second = lane >= GRID_W
        kc = jnp.where(second, lane - GRID_W, lane)
        start = jnp.clip(qc - NA_WIN_COLS // 2, 0, GRID_W - NA_WIN_COLS)
        valid = (kc >= start) & (kc < start + NA_WIN_COLS)
        idx = kc - qc + (NA_WIN_COLS - 1) + jnp.where(second, NA_RPB_COLS, 0)
        return valid, idx

    valid, idx = decode((kdim, n))
    i = lax.broadcasted_iota(jnp.int32, (kdim, n), 0)
    onehot = jnp.where(valid & (i == idx), 1.0, 0.0)
    tab = jnp.dot(r_ref[...], onehot, preferred_element_type=F32, precision=HIGHEST)
    valid_row, _ = decode((1, n))
    o_ref[...] = jnp.where(valid_row, tab, MASK_VALUE)


def _bias_table(rpb):
    heads = rpb.shape[0]
    pairs = NA_RPB_ROWS - 1
    kdim = 64
    pad = jnp.zeros((pairs, heads, kdim - 2 * NA_RPB_COLS), F32)
    rows = jnp.swapaxes(rpb, 0, 1)
    r = jnp.concatenate([rows[:-1], rows[1:], pad], axis=-1).reshape(pairs * heads, kdim)
    n = GRID_W * 2 * GRID_W
    tab = pl.pallas_call(
        _bias_table_kernel,
        out_shape=jax.ShapeDtypeStruct((pairs * heads, n), F32),
        compiler_params=pltpu.CompilerParams(vmem_limit_bytes=VMEM_LIMIT_BYTES),
    )(r)
    return tab.reshape(pairs, heads * GRID_W, 2 * GRID_W)


def _natten_kernel(q_ref, k_ref, v_ref, tab_ref, wo_ref, x_ref, o_ref, att_scr, *, rows_per_step, n_rows):
    pairs = q_ref.shape[2] // V7X_LANES
    lane = lax.broadcasted_iota(jnp.int32, (GRID_W, V7X_LANES), 1)
    low = lane < NA_HEAD_DIM
    win = NA_WIN_ROWS * GRID_W

    def row_body(rr, carry):
        r = pl.program_id(1) * rows_per_step + rr
        rs = jnp.clip(r - NA_WIN_ROWS // 2, 0, n_rows - NA_WIN_ROWS)
        d0 = rs - r + NA_WIN_ROWS - 1
        q_rows = pl.ds(pl.multiple_of(rr * GRID_W, GRID_W), GRID_W)
        k_rows = pl.ds(pl.multiple_of(rs * GRID_W, GRID_W), win)

        def scores(hp):
            cols = slice(hp * V7X_LANES, (hp + 1) * V7X_LANES)
            q2 = q_ref[0, q_rows, cols]
            zero = jnp.zeros_like(q2)
            qs = jnp.concatenate([jnp.where(low, q2, zero), jnp.where(low, zero, q2)], axis=0)
            return lax.dot_general(qs, k_ref[0, k_rows, cols], (((1,), (1,)), ((), ())),
                                   preferred_element_type=F32)

        def softmax(hp, s):
            rows = slice(hp * 2 * GRID_W, (hp + 1) * 2 * GRID_W)
            bias = jnp.concatenate(
                [tab_ref[pl.ds(d0 + 2 * p, 1), rows, :][0] for p in range(NA_WIN_ROWS // 2)], axis=1)
            s = s + bias
            m = jnp.max(s, axis=-1, keepdims=True)
            e = jnp.exp(s - m)
            return e.astype(BF16), 1.0 / jnp.sum(e, axis=-1, keepdims=True)

        def values(hp, p, linv):
            cols = slice(hp * V7X_LANES, (hp + 1) * V7X_LANES)
            o = _dot(p, v_ref[0, k_rows, cols]) * linv
            att_scr[q_rows, cols] = jnp.where(low, o[:GRID_W], o[GRID_W:]).astype(att_scr.dtype)

        s_q, p_q = {}, {}
        for t in range(pairs + 1 + NA_VALUES_LAG):
            if t < pairs:
                s_q[t] = scores(t)
            if 1 <= t <= pairs:
                p_q[t - 1] = softmax(t - 1, s_q.pop(t - 1))
            if t >= 1 + NA_VALUES_LAG:
                hp = t - 1 - NA_VALUES_LAG
                values(hp, *p_q.pop(hp))
        return carry

    lax.fori_loop(0, rows_per_step, row_body, 0, unroll=NA_ROW_UNROLL)
    o_ref[0] = x_ref[0] + _dot(att_scr[...], wo_ref[...])


def _natten(qkv, tab, w_out, x, *, rows_per_step=16):
    b, s, d3 = qkv.shape
    d = d3 // 3
    n_rows = s // GRID_W
    qs = rows_per_step * GRID_W
    once = pl.Buffered(1)
    return pl.pallas_call(
        functools.partial(_natten_kernel, rows_per_step=rows_per_step, n_rows=n_rows),
        grid=(b, n_rows // rows_per_step),
        in_specs=[
            pl.BlockSpec((1, qs, d), lambda i, j: (i, j, 0)),
            pl.BlockSpec((1, s, d), lambda i, j: (i, 0, 1)),
            pl.BlockSpec((1, s, d), lambda i, j: (i, 0, 2)),
            pl.BlockSpec(tab.shape, lambda i, j: (0, 0, 0), pipeline_mode=once),
            pl.BlockSpec((d, d), lambda i, j: (0, 0), pipeline_mode=once),
            pl.BlockSpec((1, qs, d), lambda i, j: (i, j, 0)),
        ],
        out_specs=pl.BlockSpec((1, qs, d), lambda i, j: (i, j, 0)),
        out_shape=jax.ShapeDtypeStruct((b, s, d), F32),
        scratch_shapes=[pltpu.VMEM((qs, d), BF16)],
        compiler_params=_params("parallel", "arbitrary"),
    )(qkv, qkv, qkv, tab, w_out, x)


def _block_dft_matrices(bk):
    n = 2 * bk
    row = jnp.arange(n, dtype=jnp.int32)
    keff = jnp.where(row <= bk, row, row - bk)
    quarter = jnp.where(row > bk, n // 4, 0)
    t = jnp.arange(n, dtype=jnp.int32)
    turns = (keff[:, None] * t[None, :] + quarter[:, None]) % n
    full = jnp.cos(turns.astype(F32) * (2.0 * math.pi / n))
    weight = jnp.where((row == 0) | (row == bk), 1.0 / n, 2.0 / n)
    inv = (full[:, :bk] * weight[:, None]).T
    return full[:, :bk].astype(BF16), full[:, bk:].astype(BF16), inv.astype(BF16)


def _filter_kernel(bands_ref, w1t_ref, w1c_ref, w1s_ref, b1_ref, w2_ref, b2_ref, freq_ref,
                   w3f_ref, w3r_ref, delta_ref, h_ref, hid_scr):
    seq = h_ref.shape[0] // 2
    j = lax.broadcasted_iota(jnp.int32, (seq, 1), 0)
    pos_f = j.astype(F32)
    pos_r = jnp.where(j == 0, 0, seq - j).astype(F32)

    def mlp(reverse):
        lane = lax.broadcasted_iota(jnp.int32, (1, seq), 1)
        pos = (jnp.where(lane == 0, 0, seq - lane) if reverse else lane).astype(F32)
        t = pos / (seq - 1.0)
        ang = (2.0 * math.pi) * pos / seq * bands_ref[...]
        pre = (t * w1t_ref[...]
               + jnp.dot(w1c_ref[...], jnp.cos(ang), preferred_element_type=F32, precision=HIGHEST)
               + jnp.dot(w1s_ref[...], -jnp.sin(ang), preferred_element_type=F32, precision=HIGHEST)
               + b1_ref[...])
        hid = jnp.sin(freq_ref[...] * pre)
        hid = jnp.sin(freq_ref[...] * (
            jnp.dot(w2_ref[...], hid, preferred_element_type=F32, precision=HIGHEST) + b2_ref[...]))
        return hid.T

    @pl.when((pl.program_id(0) == 0) & (pl.program_id(1) == 0))
    def _hidden():
        hid_scr[0] = mlp(False)
        hid_scr[1] = mlp(True)

    def filt(hid, w3_ref, pos):
        f = jnp.dot(hid, w3_ref[...], preferred_element_type=F32, precision=HIGHEST)
        return f * jnp.exp(-(pos / (seq - 1.0)) * delta_ref[...])

    fwd = filt(hid_scr[0], w3f_ref, pos_f)
    rev = filt(hid_scr[1], w3r_ref, pos_r)
    top = fwd + jnp.where(j == 0, rev, 0.0)
    bot = jnp.where(j == 0, 0.0, rev)
    l1 = jnp.sum(jnp.abs(top), axis=0, keepdims=True) + jnp.sum(jnp.abs(bot), axis=0, keepdims=True)
    inv = 1.0 / l1
    h_ref[0:seq, :] = (top * inv).astype(h_ref.dtype)
    h_ref[seq:2 * seq, :] = (bot * inv).astype(h_ref.dtype)


def _hyena_filters_time(seq, d, w1, b1, w2, b2, w3, freq, *, tn=512):
    nb = HYENA_BANDS
    hid = w2.shape[0]
    bands = jnp.linspace(1e-4, nb - 1, nb, dtype=F32)[:, None]
    lt = math.log(HYENA_DECAY_TARGET)
    deltas = jnp.abs(jnp.linspace(lt / HYENA_SLOW_DECAY, lt / HYENA_FAST_DECAY, d, dtype=F32))[None, :]
    small = lambda shape: pl.BlockSpec(shape, lambda o, c: (0, 0))
    cpo = d // tn
    return pl.pallas_call(
        _filter_kernel,
        grid=(2, cpo),
        in_specs=[
            small((nb, 1)), small((hid, 1)), small((hid, nb)), small((hid, nb)), small((hid, 1)),
            small((hid, hid)), small((hid, 1)), small((hid, 1)),
            pl.BlockSpec((hid, tn), lambda o, c: (0, 2 * o * cpo + c)),
            pl.BlockSpec((hid, tn), lambda o, c: (0, (2 * o + 1) * cpo + c)),
            pl.BlockSpec((1, tn), lambda o, c: (0, c)),
        ],
        out_specs=pl.BlockSpec((2 * seq, tn), lambda o, c: (0, o * cpo + c)),
        out_shape=jax.ShapeDtypeStruct((2 * seq, 2 * d), BF16),
        scratch_shapes=[pltpu.VMEM((2, seq, hid), F32)],
        compiler_params=_params("arbitrary", "arbitrary"),
    )(bands, w1[0:1].T, w1[1:1 + nb].T, w1[1 + nb:1 + 2 * nb].T, b1[:, None], w2.T, b2[:, None],
      freq[:, None], w3, w3, deltas)


def _filter_spectra_kernel(fl_ref, fr_ref, top_ref, bot_ref, o_ref):
    o_ref[0] = _dot(fl_ref[...], top_ref[...]) + _dot(fr_ref[...], bot_ref[...])


def _filter_spectra(fl, fr, filt, *, tn=1024):
    n, bk = fl.shape
    blocks = filt.shape[0] // bk
    nb = blocks // 2
    cols = filt.shape[1]
    half = pl.BlockSpec((n, bk), lambda e, j: (0, 0))
    return pl.pallas_call(
        _filter_spectra_kernel,
        grid=(2 * nb - 1, cols // tn),
        in_specs=[
            half, half,
            pl.BlockSpec((bk, tn), lambda e, j: ((e + blocks - (nb - 1)) % blocks, j)),
            pl.BlockSpec((bk, tn), lambda e, j: ((e + blocks - nb) % blocks, j)),
        ],
        out_specs=pl.BlockSpec((1, n, tn), lambda e, j: (e, 0, j)),
        out_shape=jax.ShapeDtypeStruct((2 * nb - 1, n, cols), F32),
        compiler_params=_params("parallel", "parallel"),
    )(fl, fr, filt, filt)


def _block_conv_pass(fb_ref, gb_ref, u_ref, gate_ref, skip, h_ref, o_ref, u_scr, y_scr, rc):
    n, bk = fb_ref.shape
    nb = u_ref.shape[0] // bk
    for jb in range(nb):
        u_scr[jb] = _dot(fb_ref[...], u_ref[jb * bk:(jb + 1) * bk, :])
    row0 = lax.broadcasted_iota(jnp.int32, (rc, 1), 0) == 0
    for ib in range(nb):
        for r in range(0, bk, rc):
            re = slice(r, r + rc)
            im = slice(bk + r, bk + r + rc)
            acc_re = acc_im = None
            for jb in range(nb):
                dlt = ib - jb + nb - 1
                ure, uim = u_scr[jb, re, :], u_scr[jb, im, :]
                hre, him = h_ref[dlt, re, :], h_ref[dlt, im, :]
                if r == 0:
                    t_re = ure * hre - uim * jnp.where(row0, 0.0, him)
                    t_im = jnp.where(row0, 0.0, ure) * him + uim * jnp.where(row0, him, hre)
                else:
                    t_re = ure * hre - uim * him
                    t_im = ure * him + uim * hre
                acc_re = t_re if acc_re is None else acc_re + t_re
                acc_im = t_im if acc_im is None else acc_im + t_im
            y_scr[ib, re, :] = acc_re.astype(y_scr.dtype)
            y_scr[ib, im, :] = acc_im.astype(y_scr.dtype)
        y = _dot(gb_ref[...], y_scr[ib])
        rows = slice(ib * bk, (ib + 1) * bk)
        u = u_ref[rows, :].astype(F32)
        o_ref[rows, :] = (gate_ref[rows, :].astype(F32) * (y + u * skip)).astype(o_ref.dtype)


def _hyena_conv_kernel(fb_ref, gb_ref, v_ref, x1_ref, x2_ref, skip_ref, h1_ref, h2_ref, o_ref,
                       u_scr, y_scr, z_scr, *, rc):
    _block_conv_pass(fb_ref, gb_ref, v_ref, x1_ref, skip_ref[0:1, :], h1_ref, z_scr, u_scr, y_scr, rc)
    _block_conv_pass(fb_ref, gb_ref, z_scr, x2_ref, skip_ref[1:2, :], h2_ref, o_ref, u_scr, y_scr, rc)


def _hyena_conv(fb, gb, vxx, skip, spec, batch, seq, d, *, tn=256, rc=32):
    n, bk = fb.shape
    nb = seq // bk
    cb = d // tn
    part = lambda c: pl.BlockSpec((seq, tn), lambda j, b: (b, c * cb + j))
    spectra = lambda order: pl.BlockSpec((2 * nb - 1, n, tn), lambda j, b: (0, 0, order * cb + j))
    return pl.pallas_call(
        functools.partial(_hyena_conv_kernel, rc=rc),
        grid=(cb, batch),
        in_specs=[
            pl.BlockSpec((n, bk), lambda j, b: (0, 0)),
            pl.BlockSpec((bk, n), lambda j, b: (0, 0)),
            part(0), part(1), part(2),
            pl.BlockSpec((2, tn), lambda j, b: (0, j)),
            spectra(0), spectra(1),
        ],
        out_specs=pl.BlockSpec((seq, tn), lambda j, b: (b, j)),
        out_shape=jax.ShapeDtypeStruct((batch * seq, d), BF16),
        scratch_shapes=[pltpu.VMEM((nb, n, tn), F32), pltpu.VMEM((nb, n, tn), BF16),
                        pltpu.VMEM((seq, tn), BF16)],
        compiler_params=_params("parallel", "arbitrary"),
    )(fb, gb, vxx, vxx, vxx, skip, spec, spec)


def kernel(x, norm_mix_g, norm_ffn_g, a_w_in, a_conv_w, a_w_out, b_w_qkv, b_q_norm_g, b_k_norm_g,
           b_rpb, b_w_out, c_w_in, c_short_w, c_short_b, c_f_w1, c_f_b1, c_f_w2, c_f_b2, c_f_w3,
           c_f_freq, c_f_skip, c_w_out, f_w13, f_w2):
    batch, seq, d = x.shape
    depth = norm_mix_g.shape[0]
    heads = d // NA_HEAD_DIM
    h = x.reshape(batch * seq, d)
    bf = lambda w: w.astype(BF16)
    a_w_in, a_w_out, f_w13, f_w2 = bf(a_w_in), bf(a_w_out), bf(f_w13), bf(f_w2)
    ia = ib = ic = 0
    for i in range(depth):
        g_mix = norm_mix_g[i][None, :]
        out_proj = ()
        kind = i % 3
        if kind == 0:
            h = _sconv_mixer(h, g_mix, a_w_in, a_conv_w[ia], a_w_out, ia, seq)
            ia += 1
        elif kind == 1:
            head_gain = jnp.concatenate([
                jnp.tile(b_q_norm_g[ib], heads) * (NA_HEAD_DIM ** -0.5),
                jnp.tile(b_k_norm_g[ib], heads)])[None, :]
            qkv = _qkv(h, g_mix, bf(b_w_qkv[ib]), head_gain)
            h = _natten(qkv.reshape(batch, seq, 3 * d), _bias_table(b_rpb[ib]), bf(b_w_out[ib]),
                        h.reshape(batch, seq, d)).reshape(batch * seq, d)
            ib += 1
        else:
            fl, fr, gb = _block_dft_matrices(seq // HYENA_CONV_BLOCKS)
            filt = _hyena_filters_time(seq, d, c_f_w1[ic], c_f_b1[ic], c_f_w2[ic], c_f_b2[ic],
                                       c_f_w3[ic], c_f_freq[ic])
            spec = _filter_spectra(fl, fr, filt)
            vxx = _proj_conv(h, g_mix, bf(c_w_in[ic]), c_short_w[ic], c_short_b[ic][None, :], seq)
            z = _hyena_conv(fl, gb, vxx, c_f_skip[ic], spec, batch, seq, d)
            out_proj = (z, bf(c_w_out[ic]))
            ic += 1
        h = _ffn(h, norm_ffn_g[i][None, :], f_w13, f_w2, i, *out_proj)
    return h.reshape(batch, seq, d)
```
